```python
import math
import jax, jax.numpy as jnp
from jax import lax
import numpy as np

D_MODEL = 1024
BATCH = 8
SEQ = 4096
DEPTH = 2

GRID_W = 64
CTX_LEN = 256
N_MIXERS = 2
EPS = 1e-6
S5_GC = 16
S5_G = D_MODEL // S5_GC
S5_P = 64
MLA_H = 16
MLA_NOPE = 64
MLA_ROPE = 32
MLA_V = 64
MLA_QL = D_MODEL // 2
MLA_KVL = D_MODEL // 4
MLA_QK = MLA_NOPE + MLA_ROPE
ROPE_BASE = 10000.0
Q_BLOCK = 128
FFN_HIDDEN = ((8 * D_MODEL + 3 * 256 - 1) // (3 * 256)) * 256

kernel_name = 'hybrid_s5_mla_adaln_prefix_trunk'


def rmsnorm(x, g):
    xf = x.astype(jnp.float32)
    y = xf * lax.rsqrt(jnp.mean(xf * xf, axis=-1, keepdims=True) + EPS)
    return (y * g.astype(jnp.float32)).astype(x.dtype)


def modnorm(x, g, shift, scale):
    return rmsnorm(x, g) * (1 + scale[:, None, :]) + shift[:, None, :]


def swiglu(h, w_in, w_out):
    gu = h @ w_in
    g, u = gu[..., :FFN_HIDDEN], gu[..., FFN_HIDDEN:]
    return (jax.nn.silu(g) * u) @ w_out


def s5_discretize(A_re, A_im, log_step, B_re, B_im):
    lr = -jnp.abs(A_re.astype(jnp.float32))
    li = A_im.astype(jnp.float32)
    dt = jnp.exp(log_step.astype(jnp.float32))[:, None]
    mag = jnp.exp(lr * dt)
    ar = mag * jnp.cos(li * dt)
    ai = mag * jnp.sin(li * dt)
    den = lr * lr + li * li
    cr = ((ar - 1) * lr + ai * li) / den
    ci = (ai * lr - (ar - 1) * li) / den
    Br = B_re.astype(jnp.float32)
    Bi = B_im.astype(jnp.float32)
    bbr = cr[..., None] * Br - ci[..., None] * Bi
    bbi = cr[..., None] * Bi + ci[..., None] * Br
    return ar, ai, bbr, bbi


def s5_combine(e1, e2):
    a1r, a1i, b1r, b1i = e1
    a2r, a2i, b2r, b2i = e2
    ar = a1r * a2r - a1i * a2i
    ai = a1r * a2i + a1i * a2r
    br = a2r * b1r - a2i * b1i + b2r
    bi = a2r * b1i + a2i * b1r + b2i
    return ar, ai, br, bi


def s5_direction(u, h0, ar, ai, bbr, bbi, C_re, C_im, reverse, want_y):
    Bn, L, D = u.shape
    ug = u.reshape(Bn, L, S5_G, S5_GC).swapaxes(0, 1)
    br = jnp.einsum('lbgc,gpc->lbgp', ug, bbr)
    bi = jnp.einsum('lbgc,gpc->lbgp', ug, bbi)
    if h0 is not None:
        idx = L - 1 if reverse else 0
        h0r, h0i = h0
        br = br.at[idx].add(ar * h0r - ai * h0i)
        bi = bi.at[idx].add(ar * h0i + ai * h0r)
    a_r = jnp.broadcast_to(ar, (L, 1, S5_G, S5_P))
    a_i = jnp.broadcast_to(ai, (L, 1, S5_G, S5_P))
    _, _, hr, hi = lax.associative_scan(s5_combine, (a_r, a_i, br, bi), reverse=reverse, axis=0)
    end = 0 if reverse else L - 1
    h_end = (hr[end], hi[end])
    y = None
    if want_y:
        y = (jnp.einsum('lbgp,gcp->blgc', hr, C_re.astype(jnp.float32))
             - jnp.einsum('lbgp,gcp->blgc', hi, C_im.astype(jnp.float32))).reshape(Bn, L, D)
    return y, h_end


def s5_mixer(hx, hc, w_in, A_re, A_im, log_step, B_re, B_im, C_re, C_im, D_skip, w_glu, w_out, ctx_out):
    dt = hx.dtype
    ux = (hx @ w_in).astype(jnp.float32)
    uc = (hc @ w_in).astype(jnp.float32)
    Dk = D_skip.astype(jnp.float32)
    yx = Dk * ux
    yc = Dk * uc if ctx_out else None
    for d in range(2):
        rev = d == 1
        ar, ai, bbr, bbi = s5_discretize(A_re[d], A_im[d], log_step[d], B_re[d], B_im[d])
        yc_d, hc_end = s5_direction(uc, None, ar, ai, bbr, bbi, C_re[d], C_im[d], rev, ctx_out)
        yx_d, _ = s5_direction(ux, hc_end, ar, ai, bbr, bbi, C_re[d], C_im[d], rev, True)
        yx = yx + yx_d
        if ctx_out:
            yc = yc + yc_d

    def glu_out(y):
        z = jax.nn.gelu(y).astype(dt) @ w_glu
        v, g = z[..., :D_MODEL], z[..., D_MODEL:]
        return (v * jax.nn.sigmoid(g)) @ w_out

    return glu_out(yx), (glu_out(yc) if ctx_out else None)


def axial_rope_tables(L):
    rows = L // GRID_W
    row = jnp.repeat(jnp.arange(rows, dtype=jnp.float32), GRID_W)
    col = jnp.tile(jnp.arange(GRID_W, dtype=jnp.float32), rows)
    axis_dim = MLA_ROPE // 2
    inv_freq = ROPE_BASE ** (-jnp.arange(0, axis_dim, 2, dtype=jnp.float32) / axis_dim)
    ang = jnp.stack([row[:, None] * inv_freq, col[:, None] * inv_freq], axis=1)
    return jnp.cos(ang), jnp.sin(ang)


def apply_axial_rope(t, cos, sin):
    nope, pe = t[..., :MLA_NOPE], t[..., MLA_NOPE:]
    pe = pe.reshape(*pe.shape[:-1], 2, 2, MLA_ROPE // 4)
    x1, x2 = pe[..., 0, :], pe[..., 1, :]
    c = cos[None, :, None].astype(t.dtype)
    s = sin[None, :, None].astype(t.dtype)
    rot = jnp.stack([x1 * c - x2 * s, x2 * c + x1 * s], axis=-2).reshape(*t.shape[:-1], MLA_ROPE)
    return jnp.concatenate([nope, rot], axis=-1)


def mla_kv(proj, g_kva, w_kv_b, g_k):
    Bn, L, _ = proj.shape
    c_kv = proj[..., MLA_QL:MLA_QL + MLA_KVL]
    k_pe = proj[..., MLA_QL + MLA_KVL:]
    kv = (rmsnorm(c_kv, g_kva) @ w_kv_b).reshape(Bn, L, MLA_H, MLA_NOPE + MLA_V)
    k_nope, v = kv[..., :MLA_NOPE], kv[..., MLA_NOPE:]
    k = jnp.concatenate([k_nope, jnp.broadcast_to(k_pe[:, :, None, :], (Bn, L, MLA_H, MLA_ROPE))], axis=-1)
    return rmsnorm(k, g_k), v


def mla_q(proj, g_qa, w_q_b, g_q):
    Bn, L, _ = proj.shape
    q = (rmsnorm(proj[..., :MLA_QL], g_qa) @ w_q_b).reshape(Bn, L, MLA_H, MLA_QK)
    return rmsnorm(q, g_q)


def softmax_attend(q, k, v):
    s = jnp.einsum('bqhd,bkhd->bhqk', q, k).astype(jnp.float32) * (MLA_QK ** -0.5)
    p = jax.nn.softmax(s, axis=-1)
    return jnp.einsum('bhqk,bkhd->bqhd', p.astype(v.dtype), v)


def mla_mixer(hx, hc, w_in, g_qa, g_kva, w_q_b, w_kv_b, g_q, g_k, w_o, ctx_out):
    Bn, L, _ = hx.shape
    px = hx @ w_in
    pc = hc @ w_in
    cos, sin = axial_rope_tables(L)
    qx = apply_axial_rope(mla_q(px, g_qa, w_q_b, g_q), cos, sin)
    kx, vx = mla_kv(px, g_kva, w_kv_b, g_k)
    kx = apply_axial_rope(kx, cos, sin)
    kc, vc = mla_kv(pc, g_kva, w_kv_b, g_k)
    k_all = jnp.concatenate([kc, kx], axis=1)
    v_all = jnp.concatenate([vc, vx], axis=1)
    nblk = L // Q_BLOCK
    qb = qx.reshape(Bn, nblk, Q_BLOCK, MLA_H, MLA_QK).swapaxes(0, 1)
    ob = lax.map(lambda qq: softmax_attend(qq, k_all, v_all), qb)
    ox = ob.swapaxes(0, 1).reshape(Bn, L, MLA_H * MLA_V) @ w_o
    oc = None
    if ctx_out:
        qc = mla_q(pc, g_qa, w_q_b, g_q)
        oc = softmax_attend(qc, kc, vc).reshape(hc.shape[0], hc.shape[1], MLA_H * MLA_V) @ w_o
    return ox, oc


def setup_inputs(seed: int = 0) -> dict:
    key = jax.random.key(seed)
    ks = iter(jax.random.split(key, 64))
    f32 = jnp.float32
    D = D_MODEL
    NA = (DEPTH + 1) // 2
    NB = DEPTH // 2

    def nrm(shape, scale):
        return jax.random.normal(next(ks), shape, f32) * scale

    n = jnp.arange(S5_P, dtype=f32)
    return {
        'x': nrm((BATCH, SEQ, D), 1.0),
        'c': nrm((BATCH, D), 1.0),
        'ctx': nrm((BATCH, CTX_LEN, D), 1.0),
        'c_ctx': nrm((D,), 1.0),
        'mod_w': nrm((DEPTH, D, 6 * D), 0.5 * D ** -0.5),
        'mod_b': nrm((DEPTH, 6 * D), 0.02),
        'norm_mix': 1.0 + nrm((DEPTH, D), 0.02),
        'norm_ffn': 1.0 + nrm((DEPTH, D), 0.02),
        'ffn_w_in': nrm((DEPTH, D, 2 * FFN_HIDDEN), D ** -0.5),
        'ffn_w_out': nrm((DEPTH, FFN_HIDDEN, D), FFN_HIDDEN ** -0.5),
        's5_w_in': nrm((NA, D, D), D ** -0.5),
        's5_A_re': -0.5 + nrm((NA, 2, S5_G, S5_P), 0.01),
        's5_A_im': math.pi * n + nrm((NA, 2, S5_G, S5_P), 0.01),
        's5_log_step': jax.random.uniform(next(ks), (NA, 2, S5_G), f32, math.log(1e-3), math.log(1e-1)),
        's5_B_re': nrm((NA, 2, S5_G, S5_P, S5_GC), (2 * S5_GC) ** -0.5),
        's5_B_im': nrm((NA, 2, S5_G, S5_P, S5_GC), (2 * S5_GC) ** -0.5),
        's5_C_re': nrm((NA, 2, S5_G, S5_GC, S5_P), S5_P ** -0.5),
        's5_C_im': nrm((NA, 2, S5_G, S5_GC, S5_P), S5_P ** -0.5),
        's5_D': nrm((NA, D), 1.0),
        's5_w_glu': nrm((NA, D, 2 * D), D ** -0.5),
        's5_w_out': nrm((NA, D, D), D ** -0.5),
        'mla_w_in': nrm((NB, D, MLA_QL + MLA_KVL + MLA_ROPE), D ** -0.5),
        'mla_q_a_norm': 1.0 + nrm((NB, MLA_QL), 0.02),
        'mla_kv_a_norm': 1.0 + nrm((NB, MLA_KVL), 0.02),
        'mla_w_q_b': nrm((NB, MLA_QL, MLA_H * MLA_QK), MLA_QL ** -0.5),
        'mla_w_kv_b': nrm((NB, MLA_KVL, MLA_H * (MLA_NOPE + MLA_V)), MLA_KVL ** -0.5),
        'mla_q_norm': 1.0 + nrm((NB, MLA_QK), 0.02),
        'mla_k_norm': 1.0 + nrm((NB, MLA_QK), 0.02),
        'mla_w_o': nrm((NB, MLA_H * MLA_V, D), (MLA_H * MLA_V) ** -0.5),
    }


def reference(x, c, ctx, c_ctx, mod_w, mod_b, norm_mix, norm_ffn, ffn_w_in, ffn_w_out,
              s5_w_in, s5_A_re, s5_A_im, s5_log_step, s5_B_re, s5_B_im, s5_C_re, s5_C_im, s5_D,
              s5_w_glu, s5_w_out,
              mla_w_in, mla_q_a_norm, mla_kv_a_norm, mla_w_q_b, mla_w_kv_b, mla_q_norm, mla_k_norm, mla_w_o):
    D = D_MODEL
    for i in range(DEPTH):
        last = i == DEPTH - 1
        j = i // N_MIXERS
        mod = jax.nn.silu(c) @ mod_w[i] + mod_b[i]
        mod_c = jax.nn.silu(c_ctx)[None, :] @ mod_w[i] + mod_b[i]
        sh_m, sc_m, gt_m, sh_f, sc_f, gt_f = [mod[:, k * D:(k + 1) * D] for k in range(6)]
        csh_m, csc_m, cgt_m, csh_f, csc_f, cgt_f = [mod_c[:, k * D:(k + 1) * D] for k in range(6)]
        hx = modnorm(x, norm_mix[i], sh_m, sc_m)
        hc = modnorm(ctx, norm_mix[i], csh_m, csc_m)
        if i % N_MIXERS == 0:
            ox, oc = s5_mixer(hx, hc, s5_w_in[j], s5_A_re[j], s5_A_im[j], s5_log_step[j],
                              s5_B_re[j], s5_B_im[j], s5_C_re[j], s5_C_im[j], s5_D[j],
                              s5_w_glu[j], s5_w_out[j], not last)
        else:
            ox, oc = mla_mixer(hx, hc, mla_w_in[j], mla_q_a_norm[j], mla_kv_a_norm[j], mla_w_q_b[j],
                               mla_w_kv_b[j], mla_q_norm[j], mla_k_norm[j], mla_w_o[j], not last)
        x = x + gt_m[:, None, :] * ox.astype(x.dtype)
        x = x + gt_f[:, None, :] * swiglu(modnorm(x, norm_ffn[i], sh_f, sc_f), ffn_w_in[i], ffn_w_out[i])
        if not last:
            ctx = ctx + cgt_m[:, None, :] * oc.astype(ctx.dtype)
            ctx = ctx + cgt_f[:, None, :] * swiglu(modnorm(ctx, norm_ffn[i], csh_f, csc_f), ffn_w_in[i], ffn_w_out[i])
    return x
```

```python
import functools
import math

import jax
import jax.numpy as jnp
import numpy as np
from jax import lax
from jax.experimental import pallas as pl
from jax.experimental.pallas import tpu as pltpu

F32 = jnp.float32
BF16 = jnp.bfloat16

D_MODEL = 1024
EPS = 1e-6
S5_GC = 16
S5_G = D_MODEL // S5_GC
S5_P = 64
S5_STATE = S5_G * S5_P
S5_GROUPS_PER_TILE = 16
S5_TILE_CH = S5_GROUPS_PER_TILE * S5_GC
S5_TILE_ST = S5_GROUPS_PER_TILE * S5_P
S5_NTILES = S5_G // S5_GROUPS_PER_TILE
S5_CHUNK_T = 32
MLA_H = 16
MLA_NOPE = 64
MLA_ROPE = 32
MLA_V = 64
MLA_QL = D_MODEL // 2
MLA_KVL = D_MODEL // 4
MLA_QK = MLA_NOPE + MLA_ROPE
HEAD_SLOT = 128
GRID_W = 64
ROPE_BASE = 10000.0
FFN_HIDDEN = 2816
FFN_CHUNK = 256
ROW_BLOCK = 256
ATTN_Q_BLOCK = 256
HEADS_PER_STEP = 2

VMEM_LIMIT = 56 * 1024 * 1024


def _const_spec(shape):
    nd = len(shape)
    return pl.BlockSpec(shape, lambda *_: (0,) * nd, pipeline_mode=pl.Buffered(1))


def _rms(x, g):
    ms = jnp.mean(x * x, axis=-1, keepdims=True)
    return x * lax.rsqrt(ms + EPS) * g


def _modnorm(x, g, shift, scale):
    return _rms(x, g) * (1.0 + scale) + shift


def _bdot(a, b):
    return jnp.dot(a, b, preferred_element_type=F32)


def _silu(x):
    return x * jax.nn.sigmoid(x)


def _gelu_tanh(x):
    c = math.sqrt(2.0 / math.pi)
    return 0.5 * x * (1.0 + jnp.tanh(c * (x + 0.044715 * (x * x * x))))


def _mod_kernel(c_ref, w_ref, b_ref, o_ref):
    a = _silu(c_ref[...]).astype(BF16)
    o_ref[...] = _bdot(a, w_ref[...].astype(BF16)) + b_ref[...]


def _modulation(cvec, mod_w, mod_b):
    depth, d, n = mod_w.shape
    nb = 1536
    return pl.pallas_call(
        _mod_kernel,
        grid=(depth, n // nb),
        in_specs=[
            pl.BlockSpec((16, d), lambda i, j: (0, 0)),
            pl.BlockSpec((None, d, nb), lambda i, j: (i, 0, j)),
            pl.BlockSpec((None, 1, nb), lambda i, j: (i, 0, j)),
        ],
        out_specs=pl.BlockSpec((None, 16, nb), lambda i, j: (i, 0, j)),
        out_shape=jax.ShapeDtypeStruct((depth, 16, n), F32),
        compiler_params=pltpu.CompilerParams(vmem_limit_bytes=VMEM_LIMIT),
        name="modulation",
    )(cvec, mod_w, mod_b.reshape(depth, 1, n))


def _s5_scan_chunk(x_ref, mod_ref, g_ref, win_ref, perm_ref, wb_ref, a_ref, wc_ref,
                   bbuf, hstate, *, reverse):
    nb, lc, d = x_ref.shape
    rows = nb * lc

    @pl.when(pl.program_id(0) == 0)
    def _():
        hstate[...] = jnp.zeros_like(hstate)

    shift = mod_ref[:, :, 0:d]
    scale = mod_ref[:, :, d:2 * d]
    h = _modnorm(x_ref[...], g_ref[...], shift, scale)
    hb = h.reshape(rows, d).astype(BF16)
    htb = _bdot(perm_ref[...], hb).astype(BF16)
    u = _bdot(htb, win_ref[...])
    ub = u.astype(BF16)
    for j in range(S5_NTILES):
        bj = _bdot(ub[:, j * S5_TILE_CH:(j + 1) * S5_TILE_CH], wb_ref[j])
        bbuf[:, j * S5_TILE_ST:(j + 1) * S5_TILE_ST] = bj[:, :S5_TILE_ST]
        bbuf[:, S5_STATE + j * S5_TILE_ST:S5_STATE + (j + 1) * S5_TILE_ST] = bj[:, S5_TILE_ST:]

    for j in range(S5_NTILES):
        cr = pl.ds(j * S5_TILE_ST, S5_TILE_ST)
        ci = pl.ds(S5_STATE + j * S5_TILE_ST, S5_TILE_ST)
        ar = jnp.broadcast_to(a_ref[0:1, cr], (nb, S5_TILE_ST))
        ai = jnp.broadcast_to(a_ref[1:2, cr], (nb, S5_TILE_ST))

        def body(i, carry, cr=cr, ci=ci, ar=ar, ai=ai):
            hr, hi = carry
            t = (lc - 1 - i) if reverse else i
            r = pl.ds(pl.multiple_of(t * nb, nb), nb)
            nhr = (ar * hr - ai * hi) + bbuf[r, cr]
            nhi = (ar * hi + ai * hr) + bbuf[r, ci]
            bbuf[r, cr] = nhr
            bbuf[r, ci] = nhi
            return nhr, nhi

        hr, hi = lax.fori_loop(0, lc, body, (hstate[:, cr], hstate[:, ci]), unroll=4)
        hstate[:, cr] = hr
        hstate[:, ci] = hi

    ys = []
    for j in range(S5_NTILES):
        hre = bbuf[:, j * S5_TILE_ST:(j + 1) * S5_TILE_ST].astype(BF16)
        him = bbuf[:, S5_STATE + j * S5_TILE_ST:S5_STATE + (j + 1) * S5_TILE_ST].astype(BF16)
        ys.append(_bdot(hre, wc_ref[j, 0:S5_TILE_ST, :]) + _bdot(him, wc_ref[j, S5_TILE_ST:, :]))
    return u, jnp.concatenate(ys, axis=1)


def _s5_fwd_kernel(x_ref, mod_ref, g_ref, win_ref, perm_ref, wb_ref, a_ref, wc_ref,
                   yf_ref, bbuf, hstate):
    _, y = _s5_scan_chunk(x_ref, mod_ref, g_ref, win_ref, perm_ref, wb_ref, a_ref, wc_ref,
                          bbuf, hstate, reverse=False)
    yf_ref[...] = y


def _s5_bwd_kernel(x_ref, mod_ref, g_ref, win_ref, perm_ref, wb_ref, a_ref, wc_ref,
                   yf_ref, dskip_ref, permt_ref, o_ref, bbuf, hstate):
    u, y = _s5_scan_chunk(x_ref, mod_ref, g_ref, win_ref, perm_ref, wb_ref, a_ref, wc_ref,
                          bbuf, hstate, reverse=True)
    y = y + yf_ref[...] + dskip_ref[...] * u
    gb = _gelu_tanh(y).astype(BF16)
    o_ref[...] = _bdot(permt_ref[...], gb).astype(BF16).reshape(o_ref.shape)


def _s5_mixer(xc, modtab, norm_g, w_in, perm, permt, wb, a, wc, dskip, n_ctx):
    nb, s, d = xc.shape
    lc = S5_CHUNK_T
    rows = nb * lc
    nchunks = s // lc
    ctx_chunks = n_ctx // lc

    def fwd_chunk(i):
        return i

    def bwd_chunk(i):
        return jnp.where(i < ctx_chunks, ctx_chunks - 1 - i, nchunks - 1 + ctx_chunks - i)

    def specs(chunk_of):
        return [
            pl.BlockSpec((nb, lc, d), lambda i: (0, chunk_of(i), 0)),
            pl.BlockSpec((None, nb, 1, 6 * d),
                         lambda i: ((chunk_of(i) >= ctx_chunks).astype(jnp.int32), 0, 0, 0)),
            _const_spec((1, d)),
            _const_spec((d, d)),
            _const_spec((rows, rows)),
        ]

    def dir_specs(k):
        return [
            pl.BlockSpec((None, S5_NTILES, S5_TILE_CH, 2 * S5_TILE_ST), lambda i: (k, 0, 0, 0),
                         pipeline_mode=pl.Buffered(1)),
            pl.BlockSpec((None, 2, S5_STATE), lambda i: (k, 0, 0), pipeline_mode=pl.Buffered(1)),
            pl.BlockSpec((None, S5_NTILES, 2 * S5_TILE_ST, S5_TILE_CH), lambda i: (k, 0, 0, 0),
                         pipeline_mode=pl.Buffered(1)),
        ]

    scratch = [pltpu.VMEM((rows, 2 * S5_STATE), F32), pltpu.VMEM((nb, 2 * S5_STATE), F32)]
    params = pltpu.CompilerParams(dimension_semantics=("arbitrary",), vmem_limit_bytes=VMEM_LIMIT)

    yf = pl.pallas_call(
        _s5_fwd_kernel,
        grid=(nchunks,),
        in_specs=specs(fwd_chunk) + dir_specs(0),
        out_specs=pl.BlockSpec((None, rows, d), lambda i: (i, 0, 0)),
        out_shape=jax.ShapeDtypeStruct((nchunks, rows, d), F32),
        scratch_shapes=scratch,
        compiler_params=params,
        name="s5_forward",
    )(xc, modtab, norm_g, w_in, perm, wb, a, wc)

    return pl.pallas_call(
        _s5_bwd_kernel,
        grid=(nchunks,),
        in_specs=specs(bwd_chunk) + dir_specs(1) + [
            pl.BlockSpec((None, rows, d), lambda i: (bwd_chunk(i), 0, 0)),
            _const_spec((1, d)),
            _const_spec((rows, rows)),
        ],
        out_specs=pl.BlockSpec((nb, lc, d), lambda i: (0, bwd_chunk(i), 0)),
        out_shape=jax.ShapeDtypeStruct((nb, s, d), BF16),
        scratch_shapes=scratch,
        compiler_params=params,
        name="s5_backward",
    )(xc, modtab, norm_g, w_in, perm, wb, a, wc, yf, dskip, permt)


def _ffn_residual(x1, mod_ref, g_ref, w1_ref, w2_ref):
    d = x1.shape[-1]
    hn = _modnorm(x1, g_ref[...], mod_ref[:, 3 * d:4 * d], mod_ref[:, 4 * d:5 * d]).astype(BF16)
    acc = jnp.zeros(x1.shape, F32)
    for k in range(FFN_HIDDEN // FFN_CHUNK):
        lo = k * FFN_CHUNK
        gk = _bdot(hn, w1_ref[:, lo:lo + FFN_CHUNK])
        uk = _bdot(hn, w1_ref[:, FFN_HIDDEN + lo:FFN_HIDDEN + lo + FFN_CHUNK])
        acc = acc + _bdot((_silu(gk) * uk).astype(BF16), w2_ref[lo:lo + FFN_CHUNK, :])
    return x1 + mod_ref[:, 5 * d:6 * d] * acc


def _s5_tail_kernel(x_ref, gl_ref, mod_ref, gf_ref, wglu_ref, wout_ref, w1_ref, w2_ref, o_ref):
    d = x_ref.shape[-1]
    z = _bdot(gl_ref[...], wglu_ref[...])
    gated = (z[:, :d] * jax.nn.sigmoid(z[:, d:])).astype(BF16)
    x1 = x_ref[...] + mod_ref[:, 2 * d:3 * d] * _bdot(gated, wout_ref[...])
    o_ref[...] = _ffn_residual(x1, mod_ref, gf_ref, w1_ref, w2_ref)


def _mla_tail_kernel(x_ref, at_ref, mod_ref, gf_ref, wo_ref, w1_ref, w2_ref, o_ref):
    d = x_ref.shape[-1]
    x1 = x_ref[...] + mod_ref[:, 2 * d:3 * d] * _bdot(at_ref[...], wo_ref[...])
    o_ref[...] = _ffn_residual(x1, mod_ref, gf_ref, w1_ref, w2_ref)


def _token_tail(kernel_fn, name, xc, act, modtab, norm_g, mixer_ws, w1, w2, n_ctx, skip_ctx):
    nb, s, d = xc.shape
    rb = ROW_BLOCK
    ctx_blocks = n_ctx // rb
    off = ctx_blocks if skip_ctx else 0
    nblk = s // rb - off
    tok = lambda o: pl.BlockSpec((None, rb, d), lambda b, i: (b, i + o, 0))
    return pl.pallas_call(
        kernel_fn,
        grid=(nb, nblk),
        in_specs=[
            tok(off),
            tok(0),
            pl.BlockSpec((None, None, 1, 6 * d),
                         lambda b, i: ((i + off >= ctx_blocks).astype(jnp.int32), b, 0, 0)),
            _const_spec((1, d)),
        ] + [_const_spec(w.shape) for w in mixer_ws] + [_const_spec(w1.shape), _const_spec(w2.shape)],
        out_specs=tok(0),
        out_shape=jax.ShapeDtypeStruct((nb, nblk * rb, d), F32),
        compiler_params=pltpu.CompilerParams(
            dimension_semantics=("arbitrary", "arbitrary"), vmem_limit_bytes=VMEM_LIMIT),
        name=name,
    )(xc, act, modtab, norm_g, *mixer_ws, w1, w2)


def _head_norm_rope(t, g, cos, sin):
    outs = []
    for hd in range(MLA_H):
        seg = t[:, hd * HEAD_SLOT:(hd + 1) * HEAD_SLOT]
        ms = jnp.sum(seg * seg, axis=-1, keepdims=True) * (1.0 / MLA_QK)
        seg = seg * lax.rsqrt(ms + EPS) * g
        outs.append(seg * cos + pltpu.roll(seg, HEAD_SLOT // 2, 1) * sin)
    return outs


def _mla_proj_kernel(x_ref, mod_ref, g_ref, win_ref, gqa_ref, gkva_ref, wqb_ref, wkvb_ref,
                     gq_ref, gk_ref, cos_ref, sin_ref, q_ref, k_ref, v_ref, *, q_scale):
    d = x_ref.shape[-1]
    h = _modnorm(x_ref[...], g_ref[...], mod_ref[:, 0:d], mod_ref[:, d:2 * d]).astype(BF16)
    proj = _bdot(h, win_ref[...])
    cos = cos_ref[...]
    sin = sin_ref[...]

    ql = _rms(proj[:, 0:MLA_QL], gqa_ref[...]).astype(BF16)
    q = _bdot(ql, wqb_ref[...])
    qs = _head_norm_rope(q, gq_ref[...], cos, sin)
    q_ref[...] = (jnp.concatenate(qs, axis=1) * q_scale).astype(BF16)

    ckv = _rms(proj[:, MLA_QL:MLA_QL + MLA_KVL], gkva_ref[...]).astype(BF16)
    kv = _bdot(ckv, wkvb_ref[...])
    kpe = proj[:, MLA_QL + MLA_KVL:]
    kslots = MLA_H * HEAD_SLOT
    knope = kv[:, 0:kslots]
    kfull = jnp.concatenate(
        [knope[:, hd * HEAD_SLOT:(hd + 1) * HEAD_SLOT] + kpe for hd in range(MLA_H)], axis=1)
    ks = _head_norm_rope(kfull, gk_ref[...], cos, sin)
    k_ref[...] = jnp.concatenate(ks, axis=1).astype(BF16)
    v_ref[...] = kv[:, kslots:].astype(BF16)


def _mla_project(xc, modtab, norm_g, w_in, g_qa, g_kva, w_qb, w_kvb, g_q, g_k, cos, sin, n_ctx,
                 q_scale):
    nb, s, d = xc.shape
    rb = ROW_BLOCK
    ctx_blocks = n_ctx // rb
    tok = lambda n: pl.BlockSpec((None, rb, n), lambda b, i: (b, i, 0))
    qk_n = MLA_H * HEAD_SLOT
    v_n = MLA_H * MLA_V
    return pl.pallas_call(
        functools.partial(_mla_proj_kernel, q_scale=q_scale),
        grid=(nb, s // rb),
        in_specs=[
            tok(d),
            pl.BlockSpec((None, None, 1, 6 * d),
                         lambda b, i: ((i >= ctx_blocks).astype(jnp.int32), b, 0, 0)),
            _const_spec((1, d)),
            _const_spec(w_in.shape),
            _const_spec(g_qa.shape),
            _const_spec(g_kva.shape),
            _const_spec(w_qb.shape),
            _const_spec(w_kvb.shape),
            _const_spec(g_q.shape),
            _const_spec(g_k.shape),
            pl.BlockSpec((rb, HEAD_SLOT), lambda b, i: (i, 0)),
            pl.BlockSpec((rb, HEAD_SLOT), lambda b, i: (i, 0)),
        ],
        out_specs=[tok(qk_n), tok(qk_n), tok(v_n)],
        out_shape=[
            jax.ShapeDtypeStruct((nb, s, qk_n), BF16),
            jax.ShapeDtypeStruct((nb, s, qk_n), BF16),
            jax.ShapeDtypeStruct((nb, s, v_n), BF16),
        ],
        compiler_params=pltpu.CompilerParams(
            dimension_semantics=("arbitrary", "arbitrary"), vmem_limit_bytes=VMEM_LIMIT),
        name="mla_project",
    )(xc, modtab, norm_g, w_in, g_qa, g_kva, w_qb, w_kvb, g_q, g_k, cos, sin)


def _attn_kernel(q_ref, k_ref, v_ref, o_ref):
    v = v_ref[...]
    outs = []
    for hd in range(HEADS_PER_STEP):
        q = q_ref[:, hd * HEAD_SLOT:(hd + 1) * HEAD_SLOT]
        k = k_ref[:, hd * HEAD_SLOT:(hd + 1) * HEAD_SLOT]
        s = lax.dot_general(q, k, (((1,), (1,)), ((), ())), preferred_element_type=F32)
        m = jnp.max(s, axis=1, keepdims=True)
        p = jnp.exp2(s - m)
        l = jnp.sum(p, axis=1, keepdims=True)
        outs.append(_bdot(p.astype(BF16), v) * (1.0 / l))
    lane = lax.broadcasted_iota(jnp.int32, outs[0].shape, 1)
    o_ref[...] = jnp.where(lane < MLA_V, outs[0], outs[1]).astype(BF16)


def _attention(q, k, v, n_ctx):
    nb, s, _ = q.shape
    tq = ATTN_Q_BLOCK
    nq = (s - n_ctx) // tq
    qoff = n_ctx // tq
    hp = MLA_H // HEADS_PER_STEP
    return pl.pallas_call(
        _attn_kernel,
        grid=(nb, hp, nq),
        in_specs=[
            pl.BlockSpec((None, tq, HEADS_PER_STEP * HEAD_SLOT), lambda b, h, i: (b, i + qoff, h)),
            pl.BlockSpec((None, s, HEADS_PER_STEP * HEAD_SLOT), lambda b, h, i: (b, 0, h)),
            pl.BlockSpec((None, s, HEADS_PER_STEP * MLA_V), lambda b, h, i: (b, 0, h)),
        ],
        out_specs=pl.BlockSpec((None, tq, HEADS_PER_STEP * MLA_V), lambda b, h, i: (b, i, h)),
        out_shape=jax.ShapeDtypeStruct((nb, nq * tq, MLA_H * MLA_V), BF16),
        compiler_params=pltpu.CompilerParams(
            dimension_semantics=("arbitrary", "arbitrary", "arbitrary"),
            vmem_limit_bytes=VMEM_LIMIT),
        name="mla_attention",
    )(q, k, v)


def _s5_discretize(A_re, A_im, log_step, B_re, B_im):
    lr = -jnp.abs(A_re)
    li = A_im
    dt = jnp.exp(log_step)[:, None]
    mag = jnp.exp(lr * dt)
    ar = mag * jnp.cos(li * dt)
    ai = mag * jnp.sin(li * dt)
    den = lr * lr + li * li
    cr = ((ar - 1) * lr + ai * li) / den
    ci = (ai * lr - (ar - 1) * li) / den
    bbr = cr[..., None] * B_re - ci[..., None] * B_im
    bbi = cr[..., None] * B_im + ci[..., None] * B_re
    return ar, ai, bbr, bbi


def _s5_pack(A_re, A_im, log_step, B_re, B_im, C_re, C_im):
    nt, gt = S5_NTILES, S5_GROUPS_PER_TILE
    eye = jnp.eye(gt, dtype=F32)

    def tile_in(w):
        w = w.reshape(nt, gt, S5_P, S5_GC)
        t = w.transpose(0, 1, 3, 2)[:, :, :, None, :] * eye[None, :, None, :, None]
        return t.reshape(nt, gt * S5_GC, gt * S5_P)

    def tile_out(w):
        w = w.reshape(nt, gt, S5_GC, S5_P)
        t = w.transpose(0, 1, 3, 2)[:, :, :, None, :] * eye[None, :, None, :, None]
        return t.reshape(nt, gt * S5_P, gt * S5_GC)

    wbs, avs, wcs = [], [], []
    for k in range(2):
        ar, ai, bbr, bbi = _s5_discretize(A_re[k], A_im[k], log_step[k], B_re[k], B_im[k])
        wbs.append(jnp.concatenate([tile_in(bbr), tile_in(bbi)], axis=2))
        avs.append(jnp.stack([ar.reshape(-1), ai.reshape(-1)]))
        wcs.append(jnp.concatenate([tile_out(C_re[k]), tile_out(-C_im[k])], axis=1))
    return jnp.stack(wbs).astype(BF16), jnp.stack(avs), jnp.stack(wcs).astype(BF16)


def _chunk_permutation(nb, lc):
    r = np.arange(nb * lc)
    src = (r % nb) * lc + r // nb
    p = np.zeros((nb * lc, nb * lc), np.float32)
    p[r, src] = 1.0
    return p


def _head_slot_maps():
    full = np.full((HEAD_SLOT,), MLA_QK, np.int32)
    nope = np.full((HEAD_SLOT,), MLA_NOPE, np.int32)
    pe = np.full((HEAD_SLOT,), MLA_ROPE, np.int32)
    for dim in range(MLA_NOPE):
        lane = 16 + dim if dim < 48 else 32 + dim
        full[lane] = dim
        nope[lane] = dim
    half = MLA_ROPE // 4
    for e in range(MLA_ROPE):
        axis, hf, f = e // (2 * half), (e % (2 * half)) // half, e % half
        lane = (HEAD_SLOT // 2 if hf == 0 else 0) + axis * half + f
        full[lane] = MLA_NOPE + e
        pe[lane] = e
    return full, nope, pe


def _take_padded(w, idx):
    zero = jnp.zeros(w.shape[:-1] + (1,), w.dtype)
    return jnp.take(jnp.concatenate([w, zero], axis=-1), jnp.asarray(idx), axis=-1)


def _rope_tables(n_ctx, seq):
    half = MLA_ROPE // 4
    rows = seq // GRID_W
    row = jnp.repeat(jnp.arange(rows, dtype=F32), GRID_W)
    col = jnp.tile(jnp.arange(GRID_W, dtype=F32), rows)
    axis_dim = MLA_ROPE // 2
    inv_freq = ROPE_BASE ** (-jnp.arange(0, axis_dim, 2, dtype=F32) / axis_dim)
    ang = jnp.stack([row[:, None] * inv_freq, col[:, None] * inv_freq], axis=1).reshape(seq, 2 * half)
    c, s = jnp.cos(ang), jnp.sin(ang)
    cos = jnp.ones((seq, HEAD_SLOT), F32)
    sin = jnp.zeros((seq, HEAD_SLOT), F32)
    hs = HEAD_SLOT // 2
    cos = cos.at[:, 0:2 * half].set(c).at[:, hs:hs + 2 * half].set(c)
    sin = sin.at[:, 0:2 * half].set(s).at[:, hs:hs + 2 * half].set(-s)
    cos = jnp.concatenate([jnp.ones((n_ctx, HEAD_SLOT), F32), cos], axis=0)
    sin = jnp.concatenate([jnp.zeros((n_ctx, HEAD_SLOT), F32), sin], axis=0)
    return cos, sin


def kernel(x, c, ctx, c_ctx, mod_w, mod_b, norm_mix, norm_ffn, ffn_w_in, ffn_w_out, s5_w_in, s5_A_re, s5_A_im, s5_log_step, s5_B_re, s5_B_im, s5_C_re, s5_C_im, s5_D, s5_w_glu, s5_w_out, mla_w_in, mla_q_a_norm, mla_kv_a_norm, mla_w_q_b, mla_w_kv_b, mla_q_norm, mla_k_norm, mla_w_o):
    nb, seq, d = x.shape
    n_ctx = ctx.shape[1]
    assert d == D_MODEL and nb == 8
    assert n_ctx % ROW_BLOCK == 0 and seq % ROW_BLOCK == 0 and seq % GRID_W == 0

    cvec = jnp.concatenate([c, c_ctx[None, :], jnp.zeros((16 - nb - 1, d), F32)], axis=0)
    mod = _modulation(cvec, mod_w, mod_b)
    mod_x = mod[:, :nb, None, :]
    mod_c = jnp.broadcast_to(mod[:, nb:nb + 1, None, :], mod_x.shape)
    modtab = jnp.stack([mod_c, mod_x], axis=1)

    xc = jnp.concatenate([ctx, x], axis=1)

    perm = _chunk_permutation(nb, S5_CHUNK_T)
    wb, av, wc = _s5_pack(s5_A_re[0], s5_A_im[0], s5_log_step[0], s5_B_re[0], s5_B_im[0],
                          s5_C_re[0], s5_C_im[0])
    gl = _s5_mixer(xc, modtab[0], norm_mix[0][None, :], s5_w_in[0].astype(BF16),
                   jnp.asarray(perm, BF16), jnp.asarray(perm.T, BF16), wb, av, wc,
                   s5_D[0][None, :], n_ctx)
    w1 = ffn_w_in.astype(BF16)
    w2 = ffn_w_out.astype(BF16)
    xc = _token_tail(_s5_tail_kernel, "s5_tail", xc, gl, modtab[0], norm_ffn[0][None, :],
                     [s5_w_glu[0].astype(BF16), s5_w_out[0].astype(BF16)], w1[0], w2[0],
                     n_ctx, skip_ctx=False)

    full_idx, nope_idx, pe_idx = _head_slot_maps()
    w_in = mla_w_in[0]
    w_in_slots = jnp.concatenate(
        [w_in[:, :MLA_QL + MLA_KVL], _take_padded(w_in[:, MLA_QL + MLA_KVL:], pe_idx)], axis=1)
    w_qb = _take_padded(mla_w_q_b[0].reshape(MLA_QL, MLA_H, MLA_QK), full_idx)
    w_kvb = mla_w_kv_b[0].reshape(MLA_KVL, MLA_H, MLA_NOPE + MLA_V)
    w_kvb_slots = jnp.concatenate(
        [_take_padded(w_kvb[:, :, :MLA_NOPE], nope_idx).reshape(MLA_KVL, MLA_H * HEAD_SLOT),
         w_kvb[:, :, MLA_NOPE:].reshape(MLA_KVL, MLA_H * MLA_V)], axis=1)
    cos, sin = _rope_tables(n_ctx, seq)
    q_scale = (MLA_QK ** -0.5) * math.log2(math.e)
    q, k, v = _mla_project(
        xc, modtab[1], norm_mix[1][None, :], w_in_slots.astype(BF16),
        mla_q_a_norm[0][None, :], mla_kv_a_norm[0][None, :],
        w_qb.reshape(MLA_QL, MLA_H * HEAD_SLOT).astype(BF16), w_kvb_slots.astype(BF16),
        _take_padded(mla_q_norm[0], full_idx)[None, :], _take_padded(mla_k_norm[0], full_idx)[None, :],
        cos, sin, n_ctx, q_scale)
    att = _attention(q, k, v, n_ctx)
    return _token_tail(_mla_tail_kernel, "mla_tail", xc, att, modtab[1], norm_ffn[1][None, :],
                       [mla_w_o[0].astype(BF16)], w1[1], w2[1], n_ctx, skip_ctx=True)
```

```python
import functools
import math

import jax
import jax.numpy as jnp
import numpy as np
from jax import lax
from jax.experimental import pallas as pl
from jax.experimental.pallas import tpu as pltpu

F32 = jnp.float32
BF16 = jnp.bfloat16

D_MODEL = 1024
EPS = 1e-6
S5_GC = 16
S5_G = D_MODEL // S5_GC
S5_P = 64
S5_STATE = S5_G * S5_P
S5_GROUPS_PER_TILE = 16
S5_TILE_CH = S5_GROUPS_PER_TILE * S5_GC
S5_TILE_ST = S5_GROUPS_PER_TILE * S5_P
S5_NTILES = S5_G // S5_GROUPS_PER_TILE
S5_CHUNK_T = 32
MLA_H = 16
MLA_NOPE = 64
MLA_ROPE = 32
MLA_V = 64
MLA_QL = D_MODEL // 2
MLA_KVL = D_MODEL // 4
MLA_QK = MLA_NOPE + MLA_ROPE
HEAD_SLOT = 128
GRID_W = 64
ROPE_BASE = 10000.0
FFN_HIDDEN = 2816
FFN_CHUNK = 256
ROW_BLOCK = 256
ATTN_Q_BLOCK = 256
ATTN_KEY_CHUNK = 1024
PAD_LANE = MLA_QK
VT_ROWS = MLA_V + 16
SCORE_BOUND_LIMIT = 50.0
HEADS_PER_STEP = 2

VMEM_LIMIT = 56 * 1024 * 1024


def _const_spec(shape):
    nd = len(shape)
    return pl.BlockSpec(shape, lambda *_: (0,) * nd, pipeline_mode=pl.Buffered(1))


def _rms(x, g):
    ms = jnp.mean(x * x, axis=-1, keepdims=True)
    return x * lax.rsqrt(ms + EPS) * g


def _modnorm(x, g, shift, scale):
    return _rms(x, g) * (1.0 + scale) + shift


def _bdot(a, b):
    return jnp.dot(a, b, preferred_element_type=F32)


def _silu(x):
    return x * jax.nn.sigmoid(x)


def _gelu_tanh(x):
    c = math.sqrt(2.0 / math.pi)
    return 0.5 * x * (1.0 + jnp.tanh(c * (x + 0.044715 * (x * x * x))))


def _mod_kernel(c_ref, w_ref, b_ref, o_ref):
    a = _silu(c_ref[...]).astype(BF16)
    o_ref[...] = _bdot(a, w_ref[...].astype(BF16)) + b_ref[...]


def _modulation(cvec, mod_w, mod_b):
    depth, d, n = mod_w.shape
    nb = 1536
    return pl.pallas_call(
        _mod_kernel,
        grid=(depth, n // nb),
        in_specs=[
            pl.BlockSpec((16, d), lambda i, j: (0, 0)),
            pl.BlockSpec((None, d, nb), lambda i, j: (i, 0, j)),
            pl.BlockSpec((None, 1, nb), lambda i, j: (i, 0, j)),
        ],
        out_specs=pl.BlockSpec((None, 16, nb), lambda i, j: (i, 0, j)),
        out_shape=jax.ShapeDtypeStruct((depth, 16, n), F32),
        compiler_params=pltpu.CompilerParams(vmem_limit_bytes=VMEM_LIMIT),
        name="modulation",
    )(cvec, mod_w, mod_b.reshape(depth, 1, n))


def _s5_scan_chunk(x_ref, mod_ref, g_ref, win_ref, perm_ref, wb_ref, a_ref, wc_ref,
                   bbuf, hstate, *, reverse):
    nb, lc, d = x_ref.shape
    rows = nb * lc

    @pl.when(pl.program_id(0) == 0)
    def _():
        hstate[...] = jnp.zeros_like(hstate)

    shift = mod_ref[:, :, 0:d]
    scale = mod_ref[:, :, d:2 * d]
    h = _modnorm(x_ref[...], g_ref[...], shift, scale)
    hb = h.reshape(rows, d).astype(BF16)
    htb = _bdot(perm_ref[...], hb).astype(BF16)
    u = _bdot(htb, win_ref[...])
    ub = u.astype(BF16)
    for j in range(S5_NTILES):
        bj = _bdot(ub[:, j * S5_TILE_CH:(j + 1) * S5_TILE_CH], wb_ref[j])
        bbuf[:, j * S5_TILE_ST:(j + 1) * S5_TILE_ST] = bj[:, :S5_TILE_ST]
        bbuf[:, S5_STATE + j * S5_TILE_ST:S5_STATE + (j + 1) * S5_TILE_ST] = bj[:, S5_TILE_ST:]

    for j in range(S5_NTILES):
        cr = pl.ds(j * S5_TILE_ST, S5_TILE_ST)
        ci = pl.ds(S5_STATE + j * S5_TILE_ST, S5_TILE_ST)
        ar = jnp.broadcast_to(a_ref[0:1, cr], (nb, S5_TILE_ST))
        ai = jnp.broadcast_to(a_ref[1:2, cr], (nb, S5_TILE_ST))

        def body(i, carry, cr=cr, ci=ci, ar=ar, ai=ai):
            hr, hi = carry
            t = (lc - 1 - i) if reverse else i
            r = pl.ds(pl.multiple_of(t * nb, nb), nb)
            nhr = (ar * hr - ai * hi) + bbuf[r, cr]
            nhi = (ar * hi + ai * hr) + bbuf[r, ci]
            bbuf[r, cr] = nhr
            bbuf[r, ci] = nhi
            return nhr, nhi

        hr, hi = lax.fori_loop(0, lc, body, (hstate[:, cr], hstate[:, ci]), unroll=4)
        hstate[:, cr] = hr
        hstate[:, ci] = hi

    ys = []
    for j in range(S5_NTILES):
        hre = bbuf[:, j * S5_TILE_ST:(j + 1) * S5_TILE_ST].astype(BF16)
        him = bbuf[:, S5_STATE + j * S5_TILE_ST:S5_STATE + (j + 1) * S5_TILE_ST].astype(BF16)
        ys.append(_bdot(hre, wc_ref[j, 0:S5_TILE_ST, :]) + _bdot(him, wc_ref[j, S5_TILE_ST:, :]))
    return u, jnp.concatenate(ys, axis=1)


def _s5_fwd_kernel(x_ref, mod_ref, g_ref, win_ref, perm_ref, wb_ref, a_ref, wc_ref,
                   yf_ref, bbuf, hstate):
    _, y = _s5_scan_chunk(x_ref, mod_ref, g_ref, win_ref, perm_ref, wb_ref, a_ref, wc_ref,
                          bbuf, hstate, reverse=False)
    yf_ref[...] = y


def _s5_bwd_kernel(x_ref, mod_ref, g_ref, win_ref, perm_ref, wb_ref, a_ref, wc_ref,
                   yf_ref, dskip_ref, permt_ref, o_ref, bbuf, hstate):
    u, y = _s5_scan_chunk(x_ref, mod_ref, g_ref, win_ref, perm_ref, wb_ref, a_ref, wc_ref,
                          bbuf, hstate, reverse=True)
    y = y + yf_ref[...] + dskip_ref[...] * u
    gb = _gelu_tanh(y).astype(BF16)
    o_ref[...] = _bdot(permt_ref[...], gb).astype(BF16).reshape(o_ref.shape)


def _s5_mixer(xc, modtab, norm_g, w_in, perm, permt, wb, a, wc, dskip, n_ctx):
    nb, s, d = xc.shape
    lc = S5_CHUNK_T
    rows = nb * lc
    nchunks = s // lc
    ctx_chunks = n_ctx // lc

    def fwd_chunk(i):
        return i

    def bwd_chunk(i):
        return jnp.where(i < ctx_chunks, ctx_chunks - 1 - i, nchunks - 1 + ctx_chunks - i)

    def specs(chunk_of):
        return [
            pl.BlockSpec((nb, lc, d), lambda i: (0, chunk_of(i), 0)),
            pl.BlockSpec((None, nb, 1, 6 * d),
                         lambda i: ((chunk_of(i) >= ctx_chunks).astype(jnp.int32), 0, 0, 0)),
            _const_spec((1, d)),
            _const_spec((d, d)),
            _const_spec((rows, rows)),
        ]

    def dir_specs(k):
        return [
            pl.BlockSpec((None, S5_NTILES, S5_TILE_CH, 2 * S5_TILE_ST), lambda i: (k, 0, 0, 0),
                         pipeline_mode=pl.Buffered(1)),
            pl.BlockSpec((None, 2, S5_STATE), lambda i: (k, 0, 0), pipeline_mode=pl.Buffered(1)),
            pl.BlockSpec((None, S5_NTILES, 2 * S5_TILE_ST, S5_TILE_CH), lambda i: (k, 0, 0, 0),
                         pipeline_mode=pl.Buffered(1)),
        ]

    scratch = [pltpu.VMEM((rows, 2 * S5_STATE), F32), pltpu.VMEM((nb, 2 * S5_STATE), F32)]
    params = pltpu.CompilerParams(dimension_semantics=("arbitrary",), vmem_limit_bytes=VMEM_LIMIT)

    yf = pl.pallas_call(
        _s5_fwd_kernel,
        grid=(nchunks,),
        in_specs=specs(fwd_chunk) + dir_specs(0),
        out_specs=pl.BlockSpec((None, rows, d), lambda i: (i, 0, 0)),
        out_shape=jax.ShapeDtypeStruct((nchunks, rows, d), F32),
        scratch_shapes=scratch,
        compiler_params=params,
        name="s5_forward",
    )(xc, modtab, norm_g, w_in, perm, wb, a, wc)

    return pl.pallas_call(
        _s5_bwd_kernel,
        grid=(nchunks,),
        in_specs=specs(bwd_chunk) + dir_specs(1) + [
            pl.BlockSpec((None, rows, d), lambda i: (bwd_chunk(i), 0, 0)),
            _const_spec((1, d)),
            _const_spec((rows, rows)),
        ],
        out_specs=pl.BlockSpec((nb, lc, d), lambda i: (0, bwd_chunk(i), 0)),
        out_shape=jax.ShapeDtypeStruct((nb, s, d), BF16),
        scratch_shapes=scratch,
        compiler_params=params,
        name="s5_backward",
    )(xc, modtab, norm_g, w_in, perm, wb, a, wc, yf, dskip, permt)


def _ffn_residual(x1, mod_ref, g_ref, w1_ref, w2_ref):
    d = x1.shape[-1]
    hn = _modnorm(x1, g_ref[...], mod_ref[:, 3 * d:4 * d], mod_ref[:, 4 * d:5 * d]).astype(BF16)
    acc = jnp.zeros(x1.shape, F32)
    for k in range(FFN_HIDDEN // FFN_CHUNK):
        lo = k * FFN_CHUNK
        gk = _bdot(hn, w1_ref[:, lo:lo + FFN_CHUNK])
        uk = _bdot(hn, w1_ref[:, FFN_HIDDEN + lo:FFN_HIDDEN + lo + FFN_CHUNK])
        acc = acc + _bdot((_silu(gk) * uk).astype(BF16), w2_ref[lo:lo + FFN_CHUNK, :])
    return x1 + mod_ref[:, 5 * d:6 * d] * acc


def _s5_tail_kernel(x_ref, gl_ref, mod_ref, gf_ref, wglu_ref, wout_ref, w1_ref, w2_ref, o_ref):
    d = x_ref.shape[-1]
    z = _bdot(gl_ref[...], wglu_ref[...])
    gated = (z[:, :d] * jax.nn.sigmoid(z[:, d:])).astype(BF16)
    x1 = x_ref[...] + mod_ref[:, 2 * d:3 * d] * _bdot(gated, wout_ref[...])
    o_ref[...] = _ffn_residual(x1, mod_ref, gf_ref, w1_ref, w2_ref)


def _mla_tail_kernel(x_ref, at_ref, mod_ref, gf_ref, wo_ref, w1_ref, w2_ref, o_ref):
    d = x_ref.shape[-1]
    x1 = x_ref[...] + mod_ref[:, 2 * d:3 * d] * _bdot(at_ref[...], wo_ref[...])
    o_ref[...] = _ffn_residual(x1, mod_ref, gf_ref, w1_ref, w2_ref)


def _token_tail(kernel_fn, name, xc, act, modtab, norm_g, mixer_ws, w1, w2, n_ctx, skip_ctx):
    nb, s, d = xc.shape
    rb = ROW_BLOCK
    ctx_blocks = n_ctx // rb
    off = ctx_blocks if skip_ctx else 0
    nblk = s // rb - off
    tok = lambda o: pl.BlockSpec((None, rb, d), lambda b, i: (b, i + o, 0))
    return pl.pallas_call(
        kernel_fn,
        grid=(nb, nblk),
        in_specs=[
            tok(off),
            tok(0),
            pl.BlockSpec((None, None, 1, 6 * d),
                         lambda b, i: ((i + off >= ctx_blocks).astype(jnp.int32), b, 0, 0)),
            _const_spec((1, d)),
        ] + [_const_spec(w.shape) for w in mixer_ws] + [_const_spec(w1.shape), _const_spec(w2.shape)],
        out_specs=tok(0),
        out_shape=jax.ShapeDtypeStruct((nb, nblk * rb, d), F32),
        compiler_params=pltpu.CompilerParams(
            dimension_semantics=("arbitrary", "arbitrary"), vmem_limit_bytes=VMEM_LIMIT),
        name=name,
    )(xc, act, modtab, norm_g, *mixer_ws, w1, w2)


def _head_norm_rope(t, g, cos, sin):
    outs = []
    for hd in range(MLA_H):
        seg = t[:, hd * HEAD_SLOT:(hd + 1) * HEAD_SLOT]
        ms = jnp.sum(seg * seg, axis=-1, keepdims=True) * (1.0 / MLA_QK)
        seg = seg * lax.rsqrt(ms + EPS) * g
        outs.append(seg * cos + pltpu.roll(seg, HEAD_SLOT // 2, 1) * sin)
    return outs


def _mla_proj_kernel(x_ref, mod_ref, g_ref, win_ref, gqa_ref, gkva_ref, wqb_ref, wkvb_ref,
                     gq_ref, gk_ref, cos_ref, sin_ref, qpad_ref, kpad_ref, q_ref, k_ref, v_ref,
                     *, q_scale):
    d = x_ref.shape[-1]
    h = _modnorm(x_ref[...], g_ref[...], mod_ref[:, 0:d], mod_ref[:, d:2 * d]).astype(BF16)
    proj = _bdot(h, win_ref[...])
    cos = cos_ref[...]
    sin = sin_ref[...]

    ql = _rms(proj[:, 0:MLA_QL], gqa_ref[...]).astype(BF16)
    q = _bdot(ql, wqb_ref[...])
    qs = _head_norm_rope(q, gq_ref[...], cos, sin)
    q_ref[...] = (jnp.concatenate(qs, axis=1) * q_scale + qpad_ref[...]).astype(BF16)

    ckv = _rms(proj[:, MLA_QL:MLA_QL + MLA_KVL], gkva_ref[...]).astype(BF16)
    kv = _bdot(ckv, wkvb_ref[...])
    kpe = proj[:, MLA_QL + MLA_KVL:]
    kslots = MLA_H * HEAD_SLOT
    knope = kv[:, 0:kslots]
    kfull = jnp.concatenate(
        [knope[:, hd * HEAD_SLOT:(hd + 1) * HEAD_SLOT] + kpe for hd in range(MLA_H)], axis=1)
    ks = _head_norm_rope(kfull, gk_ref[...], cos, sin)
    k_ref[...] = (jnp.concatenate(ks, axis=1) + kpad_ref[...]).astype(BF16)
    v_ref[...] = kv[:, kslots:].astype(BF16)


def _mla_project(xc, modtab, norm_g, w_in, g_qa, g_kva, w_qb, w_kvb, g_q, g_k, cos, sin, qpad, kpad, n_ctx,
                 q_scale):
    nb, s, d = xc.shape
    rb = ROW_BLOCK
    ctx_blocks = n_ctx // rb
    tok = lambda n: pl.BlockSpec((None, rb, n), lambda b, i: (b, i, 0))
    qk_n = MLA_H * HEAD_SLOT
    v_n = MLA_H * MLA_V
    return pl.pallas_call(
        functools.partial(_mla_proj_kernel, q_scale=q_scale),
        grid=(nb, s // rb),
        in_specs=[
            tok(d),
            pl.BlockSpec((None, None, 1, 6 * d),
                         lambda b, i: ((i >= ctx_blocks).astype(jnp.int32), b, 0, 0)),
            _const_spec((1, d)),
            _const_spec(w_in.shape),
            _const_spec(g_qa.shape),
            _const_spec(g_kva.shape),
            _const_spec(w_qb.shape),
            _const_spec(w_kvb.shape),
            _const_spec(g_q.shape),
            _const_spec(g_k.shape),
            pl.BlockSpec((rb, HEAD_SLOT), lambda b, i: (i, 0)),
            pl.BlockSpec((rb, HEAD_SLOT), lambda b, i: (i, 0)),
            _const_spec(qpad.shape),
            _const_spec(kpad.shape),
        ],
        out_specs=[tok(qk_n), tok(qk_n), tok(v_n)],
        out_shape=[
            jax.ShapeDtypeStruct((nb, s, qk_n), BF16),
            jax.ShapeDtypeStruct((nb, s, qk_n), BF16),
            jax.ShapeDtypeStruct((nb, s, v_n), BF16),
        ],
        compiler_params=pltpu.CompilerParams(
            dimension_semantics=("arbitrary", "arbitrary"), vmem_limit_bytes=VMEM_LIMIT),
        name="mla_project",
    )(xc, modtab, norm_g, w_in, g_qa, g_kva, w_qb, w_kvb, g_q, g_k, cos, sin, qpad, kpad)


def _attn_kernel(q_ref, k_ref, vt_ref, o_ref, *, bounded):
    tq = q_ref.shape[0]
    nkeys = k_ref.shape[0]
    qb = q_ref[...]
    lane = lax.broadcasted_iota(jnp.int32, qb.shape, 1)
    zero = jnp.zeros_like(qb)
    qbd = jnp.concatenate([jnp.where(lane < HEAD_SLOT, qb, zero),
                           jnp.where(lane >= HEAD_SLOT, qb, zero)], axis=0)

    bounds = list(range(0, nkeys, ATTN_KEY_CHUNK)) + [nkeys]

    def scores(c):
        kc = k_ref[bounds[c]:bounds[c + 1], :]
        return lax.dot_general(kc, qbd, (((1,), (1,)), ((), ())), preferred_element_type=F32)

    nchunks = len(bounds) - 1
    m = jnp.full((1, HEADS_PER_STEP * tq), -jnp.inf, F32)
    accs = [jnp.zeros((VT_ROWS, tq), F32) for _ in range(HEADS_PER_STEP)]
    s_next = scores(0)
    for c in range(nchunks):
        s = s_next
        if c + 1 < nchunks:
            s_next = scores(c + 1)
        if not bounded:
            m_new = jnp.maximum(m, jnp.max(s, axis=0, keepdims=True))
            alpha = jnp.exp2(m - m_new)
            s = s - m_new
            m = m_new
        pb = jnp.exp2(s).astype(BF16)
        for hd in range(HEADS_PER_STEP):
            pv = _bdot(vt_ref[hd, :, bounds[c]:bounds[c + 1]], pb[:, hd * tq:(hd + 1) * tq])
            if bounded:
                accs[hd] = accs[hd] + pv
            else:
                accs[hd] = accs[hd] * alpha[:, hd * tq:(hd + 1) * tq] + pv
    outs = [a[0:MLA_V, :] * (1.0 / a[MLA_V:MLA_V + 1, :]) for a in accs]
    o_ref[...] = jnp.concatenate(outs, axis=0).T.astype(BF16)


def _attention(q, k, vt, n_ctx, bounded):
    nb, s, _ = q.shape
    tq = ATTN_Q_BLOCK
    nq = (s - n_ctx) // tq
    qoff = n_ctx // tq
    hp = MLA_H // HEADS_PER_STEP
    return pl.pallas_call(
        functools.partial(_attn_kernel, bounded=bounded),
        grid=(nb, hp, nq),
        in_specs=[
            pl.BlockSpec((None, tq, HEADS_PER_STEP * HEAD_SLOT), lambda b, h, i: (b, i + qoff, h)),
            pl.BlockSpec((None, s, HEADS_PER_STEP * HEAD_SLOT), lambda b, h, i: (b, 0, h)),
            pl.BlockSpec((None, HEADS_PER_STEP, VT_ROWS, s), lambda b, h, i: (b, h, 0, 0)),
        ],
        out_specs=pl.BlockSpec((None, tq, HEADS_PER_STEP * MLA_V), lambda b, h, i: (b, i, h)),
        out_shape=jax.ShapeDtypeStruct((nb, nq * tq, MLA_H * MLA_V), BF16),
        compiler_params=pltpu.CompilerParams(
            dimension_semantics=("arbitrary", "arbitrary", "arbitrary"),
            vmem_limit_bytes=VMEM_LIMIT),
        name="mla_attention_bounded" if bounded else "mla_attention_online",
    )(q, k, vt)


def _s5_discretize(A_re, A_im, log_step, B_re, B_im):
    lr = -jnp.abs(A_re)
    li = A_im
    dt = jnp.exp(log_step)[:, None]
    mag = jnp.exp(lr * dt)
    ar = mag * jnp.cos(li * dt)
    ai = mag * jnp.sin(li * dt)
    den = lr * lr + li * li
    cr = ((ar - 1) * lr + ai * li) / den
    ci = (ai * lr - (ar - 1) * li) / den
    bbr = cr[..., None] * B_re - ci[..., None] * B_im
    bbi = cr[..., None] * B_im + ci[..., None] * B_re
    return ar, ai, bbr, bbi


def _s5_pack(A_re, A_im, log_step, B_re, B_im, C_re, C_im):
    nt, gt = S5_NTILES, S5_GROUPS_PER_TILE
    eye = jnp.eye(gt, dtype=F32)

    def tile_in(w):
        w = w.reshape(nt, gt, S5_P, S5_GC)
        t = w.transpose(0, 1, 3, 2)[:, :, :, None, :] * eye[None, :, None, :, None]
        return t.reshape(nt, gt * S5_GC, gt * S5_P)

    def tile_out(w):
        w = w.reshape(nt, gt, S5_GC, S5_P)
        t = w.transpose(0, 1, 3, 2)[:, :, :, None, :] * eye[None, :, None, :, None]
        return t.reshape(nt, gt * S5_P, gt * S5_GC)

    wbs, avs, wcs = [], [], []
    for k in range(2):
        ar, ai, bbr, bbi = _s5_discretize(A_re[k], A_im[k], log_step[k], B_re[k], B_im[k])
        wbs.append(jnp.concatenate([tile_in(bbr), tile_in(bbi)], axis=2))
        avs.append(jnp.stack([ar.reshape(-1), ai.reshape(-1)]))
        wcs.append(jnp.concatenate([tile_out(C_re[k]), tile_out(-C_im[k])], axis=1))
    return jnp.stack(wbs).astype(BF16), jnp.stack(avs), jnp.stack(wcs).astype(BF16)


def _chunk_permutation(nb, lc):
    r = np.arange(nb * lc)
    src = (r % nb) * lc + r // nb
    p = np.zeros((nb * lc, nb * lc), np.float32)
    p[r, src] = 1.0
    return p


def _head_slot_maps():
    full = np.full((HEAD_SLOT,), MLA_QK, np.int32)
    nope = np.full((HEAD_SLOT,), MLA_NOPE, np.int32)
    pe = np.full((HEAD_SLOT,), MLA_ROPE, np.int32)
    for dim in range(MLA_NOPE):
        lane = 16 + dim if dim < 48 else 32 + dim
        full[lane] = dim
        nope[lane] = dim
    half = MLA_ROPE // 4
    for e in range(MLA_ROPE):
        axis, hf, f = e // (2 * half), (e % (2 * half)) // half, e % half
        lane = (HEAD_SLOT // 2 if hf == 0 else 0) + axis * half + f
        full[lane] = MLA_NOPE + e
        pe[lane] = e
    return full, nope, pe


def _take_padded(w, idx):
    zero = jnp.zeros(w.shape[:-1] + (1,), w.dtype)
    return jnp.take(jnp.concatenate([w, zero], axis=-1), jnp.asarray(idx), axis=-1)


def _rope_tables(n_ctx, seq):
    half = MLA_ROPE // 4
    rows = seq // GRID_W
    row = jnp.repeat(jnp.arange(rows, dtype=F32), GRID_W)
    col = jnp.tile(jnp.arange(GRID_W, dtype=F32), rows)
    axis_dim = MLA_ROPE // 2
    inv_freq = ROPE_BASE ** (-jnp.arange(0, axis_dim, 2, dtype=F32) / axis_dim)
    ang = jnp.stack([row[:, None] * inv_freq, col[:, None] * inv_freq], axis=1).reshape(seq, 2 * half)
    c, s = jnp.cos(ang), jnp.sin(ang)
    cos = jnp.ones((seq, HEAD_SLOT), F32)
    sin = jnp.zeros((seq, HEAD_SLOT), F32)
    hs = HEAD_SLOT // 2
    cos = cos.at[:, 0:2 * half].set(c).at[:, hs:hs + 2 * half].set(c)
    sin = sin.at[:, 0:2 * half].set(s).at[:, hs:hs + 2 * half].set(-s)
    cos = jnp.concatenate([jnp.ones((n_ctx, HEAD_SLOT), F32), cos], axis=0)
    sin = jnp.concatenate([jnp.zeros((n_ctx, HEAD_SLOT), F32), sin], axis=0)
    return cos, sin


def kernel(x, c, ctx, c_ctx, mod_w, mod_b, norm_mix, norm_ffn, ffn_w_in, ffn_w_out, s5_w_in, s5_A_re, s5_A_im, s5_log_step, s5_B_re, s5_B_im, s5_C_re, s5_C_im, s5_D, s5_w_glu, s5_w_out, mla_w_in, mla_q_a_norm, mla_kv_a_norm, mla_w_q_b, mla_w_kv_b, mla_q_norm, mla_k_norm, mla_w_o):
    nb, seq, d = x.shape
    n_ctx = ctx.shape[1]
    assert d == D_MODEL and nb == 8
    assert n_ctx % ROW_BLOCK == 0 and seq % ROW_BLOCK == 0 and seq % GRID_W == 0

    cvec = jnp.concatenate([c, c_ctx[None, :], jnp.zeros((16 - nb - 1, d), F32)], axis=0)
    mod = _modulation(cvec, mod_w, mod_b)
    mod_x = mod[:, :nb, None, :]
    mod_c = jnp.broadcast_to(mod[:, nb:nb + 1, None, :], mod_x.shape)
    modtab = jnp.stack([mod_c, mod_x], axis=1)

    xc = jnp.concatenate([ctx, x], axis=1)

    perm = _chunk_permutation(nb, S5_CHUNK_T)
    wb, av, wc = _s5_pack(s5_A_re[0], s5_A_im[0], s5_log_step[0], s5_B_re[0], s5_B_im[0],
                          s5_C_re[0], s5_C_im[0])
    gl = _s5_mixer(xc, modtab[0], norm_mix[0][None, :], s5_w_in[0].astype(BF16),
                   jnp.asarray(perm, BF16), jnp.asarray(perm.T, BF16), wb, av, wc,
                   s5_D[0][None, :], n_ctx)
    w1 = ffn_w_in.astype(BF16)
    w2 = ffn_w_out.astype(BF16)
    xc = _token_tail(_s5_tail_kernel, "s5_tail", xc, gl, modtab[0], norm_ffn[0][None, :],
                     [s5_w_glu[0].astype(BF16), s5_w_out[0].astype(BF16)], w1[0], w2[0],
                     n_ctx, skip_ctx=False)

    full_idx, nope_idx, pe_idx = _head_slot_maps()
    w_in = mla_w_in[0]
    w_in_slots = jnp.concatenate(
        [w_in[:, :MLA_QL + MLA_KVL], _take_padded(w_in[:, MLA_QL + MLA_KVL:], pe_idx)], axis=1)
    w_qb = _take_padded(mla_w_q_b[0].reshape(MLA_QL, MLA_H, MLA_QK), full_idx)
    w_kvb = mla_w_kv_b[0].reshape(MLA_KVL, MLA_H, MLA_NOPE + MLA_V)
    w_kvb_slots = jnp.concatenate(
        [_take_padded(w_kvb[:, :, :MLA_NOPE], nope_idx).reshape(MLA_KVL, MLA_H * HEAD_SLOT),
         w_kvb[:, :, MLA_NOPE:].reshape(MLA_KVL, MLA_H * MLA_V)], axis=1)
    cos, sin = _rope_tables(n_ctx, seq)
    q_scale = (MLA_QK ** -0.5) * math.log2(math.e)
    score_bound = 1.02 * MLA_QK * q_scale * jnp.max(jnp.abs(mla_q_norm[0])) * jnp.max(jnp.abs(mla_k_norm[0]))
    pad_onehot = jnp.tile(jnp.zeros((HEAD_SLOT,), F32).at[PAD_LANE].set(1.0), MLA_H)[None, :]
    qpad = -score_bound * pad_onehot
    kpad = pad_onehot
    q, k, v = _mla_project(
        xc, modtab[1], norm_mix[1][None, :], w_in_slots.astype(BF16),
        mla_q_a_norm[0][None, :], mla_kv_a_norm[0][None, :],
        w_qb.reshape(MLA_QL, MLA_H * HEAD_SLOT).astype(BF16), w_kvb_slots.astype(BF16),
        _take_padded(mla_q_norm[0], full_idx)[None, :], _take_padded(mla_k_norm[0], full_idx)[None, :],
        cos, sin, qpad, kpad, n_ctx, q_scale)
    s_all = v.shape[1]
    vt = jnp.concatenate(
        [v.reshape(nb, s_all, MLA_H, MLA_V).transpose(0, 2, 3, 1),
         jnp.ones((nb, MLA_H, VT_ROWS - MLA_V, s_all), BF16)], axis=2)
    att = lax.cond(score_bound <= SCORE_BOUND_LIMIT,
                   lambda: _attention(q, k, vt, n_ctx, True),
                   lambda: _attention(q, k, vt, n_ctx, False))
    return _token_tail(_mla_tail_kernel, "mla_tail", xc, att, modtab[1], norm_ffn[1][None, :],
                       [mla_w_o[0].astype(BF16)], w1[1], w2[1], n_ctx, skip_ctx=True)
```

```python
import functools
import math

import jax
import jax.numpy as jnp
import numpy as np
from jax import lax
from jax.experimental import pallas as pl
from jax.experimental.pallas import tpu as pltpu

F32 = jnp.float32
BF16 = jnp.bfloat16

D_MODEL = 1024
EPS = 1e-6
S5_GC = 16
S5_G = D_MODEL // S5_GC
S5_P = 64
S5_STATE = S5_G * S5_P
S5_GROUPS_PER_TILE = 16
S5_TILE_CH = S5_GROUPS_PER_TILE * S5_GC
S5_TILE_ST = S5_GROUPS_PER_TILE * S5_P
S5_NTILES = S5_G // S5_GROUPS_PER_TILE
S5_CHUNK_T = 32
MLA_H = 16
MLA_NOPE = 64
MLA_ROPE = 32
MLA_V = 64
MLA_QL = D_MODEL // 2
MLA_KVL = D_MODEL // 4
MLA_QK = MLA_NOPE + MLA_ROPE
HEAD_SLOT = 128
ROPE_PAIR = HEAD_SLOT // 2
ROPE_HALF = MLA_ROPE // 2
PAD_LANE = MLA_QK
SUBLANES = 8
GRID_W = 64
ROPE_BASE = 10000.0
FFN_HIDDEN = 2816
FFN_CHUNK = 256
ROW_BLOCK = 256
ATTN_Q_BLOCK = 256
ATTN_Q_STEP = 2 * ATTN_Q_BLOCK
ATTN_KEY_CHUNK = 1024
HEADS_PER_STEP = 2
VT_ROWS = MLA_V + 16
SCORE_BOUND_LIMIT = 50.0

VMEM_LIMIT = 56 * 1024 * 1024


def _const_spec(shape):
    nd = len(shape)
    return pl.BlockSpec(shape, lambda *_: (0,) * nd, pipeline_mode=pl.Buffered(1))


def _rms(x, g):
    ms = jnp.mean(x * x, axis=-1, keepdims=True)
    return x * lax.rsqrt(ms + EPS) * g


def _modnorm(x, g, shift, scale):
    return _rms(x, g) * (1.0 + scale) + shift


def _bdot(a, b):
    return jnp.dot(a, b, preferred_element_type=F32)


def _bdot_nt(a, b):
    return lax.dot_general(a, b, (((1,), (1,)), ((), ())), preferred_element_type=F32)


def _silu(x):
    return x * jax.nn.sigmoid(x)


def _gelu_tanh(x):
    c = math.sqrt(2.0 / math.pi)
    return 0.5 * x * (1.0 + jnp.tanh(c * (x + 0.044715 * (x * x * x))))


def _mod_kernel(c_ref, w_ref, b_ref, o_ref):
    a = _silu(c_ref[...]).astype(BF16)
    o_ref[...] = _bdot(a, w_ref[...].astype(BF16)) + b_ref[...]


def _modulation(cvec, mod_w, mod_b):
    depth, d, n = mod_w.shape
    nb = 1536
    return pl.pallas_call(
        _mod_kernel,
        grid=(depth, n // nb),
        in_specs=[
            pl.BlockSpec((16, d), lambda i, j: (0, 0)),
            pl.BlockSpec((None, d, nb), lambda i, j: (i, 0, j)),
            pl.BlockSpec((None, 1, nb), lambda i, j: (i, 0, j)),
        ],
        out_specs=pl.BlockSpec((None, 16, nb), lambda i, j: (i, 0, j)),
        out_shape=jax.ShapeDtypeStruct((depth, 16, n), F32),
        compiler_params=pltpu.CompilerParams(vmem_limit_bytes=VMEM_LIMIT),
        name="modulation",
    )(cvec, mod_w, mod_b.reshape(depth, 1, n))


def _s5_scan_chunk(is_ctx, x_ref, ctx_ref, mod_ref, g_ref, win_ref, perm_ref, wb_ref, a_ref, wc_ref,
                   bbuf, hstate, *, reverse):
    nb, lc, d = x_ref.shape
    rows = nb * lc

    @pl.when(pl.program_id(0) == 0)
    def _():
        hstate[...] = jnp.zeros_like(hstate)

    xin = jnp.where(is_ctx, ctx_ref[...], x_ref[...])
    shift = mod_ref[:, :, 0:d]
    scale = mod_ref[:, :, d:2 * d]
    h = _modnorm(xin, g_ref[...], shift, scale)
    hb = h.reshape(rows, d).astype(BF16)
    htb = _bdot(perm_ref[...], hb).astype(BF16)
    u = _bdot(htb, win_ref[...])
    ub = u.astype(BF16)
    for j in range(S5_NTILES):
        bj = _bdot(ub[:, j * S5_TILE_CH:(j + 1) * S5_TILE_CH], wb_ref[j])
        bbuf[:, j * S5_TILE_ST:(j + 1) * S5_TILE_ST] = bj[:, :S5_TILE_ST]
        bbuf[:, S5_STATE + j * S5_TILE_ST:S5_STATE + (j + 1) * S5_TILE_ST] = bj[:, S5_TILE_ST:]

    for j in range(S5_NTILES):
        cr = pl.ds(j * S5_TILE_ST, S5_TILE_ST)
        ci = pl.ds(S5_STATE + j * S5_TILE_ST, S5_TILE_ST)
        ar = jnp.broadcast_to(a_ref[0:1, cr], (nb, S5_TILE_ST))
        ai = jnp.broadcast_to(a_ref[1:2, cr], (nb, S5_TILE_ST))

        def body(i, carry, cr=cr, ci=ci, ar=ar, ai=ai):
            hr, hi = carry
            t = (lc - 1 - i) if reverse else i
            r = pl.ds(pl.multiple_of(t * nb, nb), nb)
            nhr = (ar * hr - ai * hi) + bbuf[r, cr]
            nhi = (ar * hi + ai * hr) + bbuf[r, ci]
            bbuf[r, cr] = nhr
            bbuf[r, ci] = nhi
            return nhr, nhi

        hr, hi = lax.fori_loop(0, lc, body, (hstate[:, cr], hstate[:, ci]), unroll=4)
        hstate[:, cr] = hr
        hstate[:, ci] = hi

    ys = []
    for j in range(S5_NTILES):
        hre = bbuf[:, j * S5_TILE_ST:(j + 1) * S5_TILE_ST].astype(BF16)
        him = bbuf[:, S5_STATE + j * S5_TILE_ST:S5_STATE + (j + 1) * S5_TILE_ST].astype(BF16)
        ys.append(_bdot(hre, wc_ref[j, 0:S5_TILE_ST, :]) + _bdot(him, wc_ref[j, S5_TILE_ST:, :]))
    return u, jnp.concatenate(ys, axis=1)


def _s5_fwd_kernel(x_ref, ctx_ref, mod_ref, g_ref, win_ref, perm_ref, wb_ref, a_ref, wc_ref,
                   yf_ref, bbuf, hstate, *, ctx_chunks):
    is_ctx = pl.program_id(0) < ctx_chunks
    _, y = _s5_scan_chunk(is_ctx, x_ref, ctx_ref, mod_ref, g_ref, win_ref, perm_ref, wb_ref, a_ref,
                          wc_ref, bbuf, hstate, reverse=False)
    yf_ref[...] = y


def _s5_bwd_kernel(x_ref, ctx_ref, mod_ref, g_ref, win_ref, perm_ref, wb_ref, a_ref, wc_ref,
                   yf_ref, dskip_ref, permt_ref, o_ref, bbuf, hstate, *, ctx_chunks):
    is_ctx = pl.program_id(0) < ctx_chunks
    u, y = _s5_scan_chunk(is_ctx, x_ref, ctx_ref, mod_ref, g_ref, win_ref, perm_ref, wb_ref, a_ref,
                          wc_ref, bbuf, hstate, reverse=True)
    y = y + yf_ref[...] + dskip_ref[...] * u
    gb = _gelu_tanh(y).astype(BF16)
    o_ref[...] = _bdot(permt_ref[...], gb).astype(BF16).reshape(o_ref.shape)


def _s5_mixer(x, ctx, modtab, norm_g, w_in, perm, permt, wb, a, wc, dskip):
    nb, seq, d = x.shape
    n_ctx = ctx.shape[1]
    lc = S5_CHUNK_T
    rows = nb * lc
    ctx_chunks = n_ctx // lc
    nchunks = ctx_chunks + seq // lc

    def fwd_chunk(i):
        return i

    def bwd_chunk(i):
        return jnp.where(i < ctx_chunks, ctx_chunks - 1 - i, nchunks - 1 + ctx_chunks - i)

    def specs(chunk_of):
        return [
            pl.BlockSpec((nb, lc, d), lambda i: (0, jnp.maximum(chunk_of(i) - ctx_chunks, 0), 0)),
            pl.BlockSpec((nb, lc, d), lambda i: (0, jnp.minimum(chunk_of(i), ctx_chunks - 1), 0)),
            pl.BlockSpec((None, nb, 1, 6 * d),
                         lambda i: ((chunk_of(i) >= ctx_chunks).astype(jnp.int32), 0, 0, 0)),
            _const_spec((1, d)),
            _const_spec((d, d)),
            _const_spec((rows, rows)),
        ]

    def dir_specs(k):
        return [
            pl.BlockSpec((None, S5_NTILES, S5_TILE_CH, 2 * S5_TILE_ST), lambda i: (k, 0, 0, 0),
                         pipeline_mode=pl.Buffered(1)),
            pl.BlockSpec((None, 2, S5_STATE), lambda i: (k, 0, 0), pipeline_mode=pl.Buffered(1)),
            pl.BlockSpec((None, S5_NTILES, 2 * S5_TILE_ST, S5_TILE_CH), lambda i: (k, 0, 0, 0),
                         pipeline_mode=pl.Buffered(1)),
        ]

    scratch = [pltpu.VMEM((rows, 2 * S5_STATE), F32), pltpu.VMEM((nb, 2 * S5_STATE), F32)]
    params = pltpu.CompilerParams(dimension_semantics=("arbitrary",), vmem_limit_bytes=VMEM_LIMIT)

    yf = pl.pallas_call(
        functools.partial(_s5_fwd_kernel, ctx_chunks=ctx_chunks),
        grid=(nchunks,),
        in_specs=specs(fwd_chunk) + dir_specs(0),
        out_specs=pl.BlockSpec((None, rows, d), lambda i: (i, 0, 0)),
        out_shape=jax.ShapeDtypeStruct((nchunks, rows, d), F32),
        scratch_shapes=scratch,
        compiler_params=params,
        name="s5_forward",
    )(x, ctx, modtab, norm_g, w_in, perm, wb, a, wc)

    return pl.pallas_call(
        functools.partial(_s5_bwd_kernel, ctx_chunks=ctx_chunks),
        grid=(nchunks,),
        in_specs=specs(bwd_chunk) + dir_specs(1) + [
            pl.BlockSpec((None, rows, d), lambda i: (bwd_chunk(i), 0, 0)),
            _const_spec((1, d)),
            _const_spec((rows, rows)),
        ],
        out_specs=pl.BlockSpec((nb, lc, d), lambda i: (0, bwd_chunk(i), 0)),
        out_shape=jax.ShapeDtypeStruct((nb, n_ctx + seq, d), BF16),
        scratch_shapes=scratch,
        compiler_params=params,
        name="s5_backward",
    )(x, ctx, modtab, norm_g, w_in, perm, wb, a, wc, yf, dskip, permt)


def _ffn_residual(x1, mod_ref, g_ref, w1_ref, w2_ref):
    d = x1.shape[-1]
    hn = _modnorm(x1, g_ref[...], mod_ref[:, 3 * d:4 * d], mod_ref[:, 4 * d:5 * d]).astype(BF16)

    def up(k):
        lo = k * FFN_CHUNK
        return (_bdot(hn, w1_ref[:, lo:lo + FFN_CHUNK]),
                _bdot(hn, w1_ref[:, FFN_HIDDEN + lo:FFN_HIDDEN + lo + FFN_CHUNK]))

    nchunks = FFN_HIDDEN // FFN_CHUNK
    acc = jnp.zeros(x1.shape, F32)
    nxt = up(0)
    for k in range(nchunks):
        gk, uk = nxt
        if k + 1 < nchunks:
            nxt = up(k + 1)
        acc = acc + _bdot((_silu(gk) * uk).astype(BF16), w2_ref[k * FFN_CHUNK:(k + 1) * FFN_CHUNK, :])
    return x1 + mod_ref[:, 5 * d:6 * d] * acc


def _s5_tail_kernel(x_ref, ctx_ref, gl_ref, mod_ref, gf_ref, wglu_ref, wout_ref, w1_ref, w2_ref,
                    o_ref, *, ctx_blocks):
    d = x_ref.shape[-1]
    xin = jnp.where(pl.program_id(1) < ctx_blocks, ctx_ref[...], x_ref[...])
    z = _bdot(gl_ref[...], wglu_ref[...])
    gated = (z[:, :d] * jax.nn.sigmoid(z[:, d:])).astype(BF16)
    x1 = xin + mod_ref[:, 2 * d:3 * d] * _bdot(gated, wout_ref[...])
    o_ref[...] = _ffn_residual(x1, mod_ref, gf_ref, w1_ref, w2_ref)


def _mla_tail_kernel(x_ref, at_ref, mod_ref, gf_ref, wo_ref, w1_ref, w2_ref, o_ref):
    d = x_ref.shape[-1]
    x1 = x_ref[...] + mod_ref[:, 2 * d:3 * d] * _bdot(at_ref[...], wo_ref[...])
    o_ref[...] = _ffn_residual(x1, mod_ref, gf_ref, w1_ref, w2_ref)


_TAIL_PARAMS = pltpu.CompilerParams(
    dimension_semantics=("arbitrary", "arbitrary"), vmem_limit_bytes=VMEM_LIMIT)


def _s5_tail(x, ctx, gl, modtab, norm_g, w_glu, w_out, w1, w2):
    nb, seq, d = x.shape
    n_ctx = ctx.shape[1]
    rb = ROW_BLOCK
    ctx_blocks = n_ctx // rb
    nblk = ctx_blocks + seq // rb
    tok = pl.BlockSpec((None, rb, d), lambda b, i: (b, i, 0))
    return pl.pallas_call(
        functools.partial(_s5_tail_kernel, ctx_blocks=ctx_blocks),
        grid=(nb, nblk),
        in_specs=[
            pl.BlockSpec((None, rb, d), lambda b, i: (b, jnp.maximum(i - ctx_blocks, 0), 0)),
            pl.BlockSpec((None, rb, d), lambda b, i: (b, jnp.minimum(i, ctx_blocks - 1), 0)),
            tok,
            pl.BlockSpec((None, None, 1, 6 * d),
                         lambda b, i: ((i >= ctx_blocks).astype(jnp.int32), b, 0, 0)),
            _const_spec((1, d)),
            _const_spec(w_glu.shape), _const_spec(w_out.shape),
            _const_spec(w1.shape), _const_spec(w2.shape),
        ],
        out_specs=tok,
        out_shape=jax.ShapeDtypeStruct((nb, nblk * rb, d), F32),
        compiler_params=_TAIL_PARAMS,
        name="s5_tail",
    )(x, ctx, gl, modtab, norm_g, w_glu, w_out, w1, w2)


def _mla_tail(xc, att, modtab, norm_g, w_o, w1, w2, n_ctx):
    nb, s, d = xc.shape
    rb = ROW_BLOCK
    off = n_ctx // rb
    nblk = s // rb - off
    tok = pl.BlockSpec((None, rb, d), lambda b, i: (b, i, 0))
    return pl.pallas_call(
        _mla_tail_kernel,
        grid=(nb, nblk),
        in_specs=[
            pl.BlockSpec((None, rb, d), lambda b, i: (b, i + off, 0)),
            tok,
            pl.BlockSpec((None, None, 1, 6 * d), lambda b, i: (1, b, 0, 0)),
            _const_spec((1, d)),
            _const_spec(w_o.shape), _const_spec(w1.shape), _const_spec(w2.shape),
        ],
        out_specs=tok,
        out_shape=jax.ShapeDtypeStruct((nb, nblk * rb, d), F32),
        compiler_params=_TAIL_PARAMS,
        name="mla_tail",
    )(xc, att, modtab, norm_g, w_o, w1, w2)


def _mla_proj_kernel(x_ref, mod_ref, g_ref, win_ref, gqa_ref, gkva_ref, wqbt_ref, wkb_ref, wvt_ref,
                     gcq_ref, gsq_ref, qpad_ref, gck_ref, gsk_ref, kpad_ref,
                     qt_ref, k_ref, vt_ref, *, ctx_blocks):
    d = x_ref.shape[-1]
    h = _modnorm(x_ref[...], g_ref[...], mod_ref[:, 0:d], mod_ref[:, d:2 * d]).astype(BF16)
    proj = _bdot(h, win_ref[...])
    inv_qk = 1.0 / MLA_QK
    lo, hi = slice(0, ROPE_HALF), slice(ROPE_PAIR, ROPE_PAIR + ROPE_HALF)

    @pl.when(pl.program_id(1) >= ctx_blocks)
    def _():
        ql = _rms(proj[:, 0:MLA_QL], gqa_ref[...]).astype(BF16)
        qt = _bdot_nt(wqbt_ref[...], ql)
        gc, gs = gcq_ref[...], gsq_ref[...]
        pad = jnp.zeros((HEAD_SLOT - MLA_QK - SUBLANES, qt.shape[1]), F32)
        for hd in range(MLA_H):
            seg = qt[hd * HEAD_SLOT:hd * HEAD_SLOT + MLA_QK, :]
            r = lax.rsqrt(jnp.sum(seg * seg, axis=0, keepdims=True) * inv_qk + EPS)
            rot = seg * gc
            rot = jnp.concatenate([rot[lo] + seg[hi] * gs[lo], rot[ROPE_HALF:ROPE_PAIR],
                                   rot[hi] + seg[lo] * gs[hi], rot[ROPE_PAIR + ROPE_HALF:]], axis=0)
            out = jnp.concatenate([rot * r, qpad_ref[...], pad], axis=0)
            qt_ref[hd * HEAD_SLOT:(hd + 1) * HEAD_SLOT, :] = out.astype(BF16)

    ckv = _rms(proj[:, MLA_QL:MLA_QL + MLA_KVL], gkva_ref[...]).astype(BF16)
    knope = _bdot(ckv, wkb_ref[...])
    kpe = proj[:, MLA_QL + MLA_KVL:]
    gck = gck_ref[...]
    rot_pe = pltpu.roll(kpe, ROPE_PAIR, 1) * gsk_ref[...]
    kpad = kpad_ref[...]
    for hd in range(MLA_H):
        t = knope[:, hd * HEAD_SLOT:(hd + 1) * HEAD_SLOT] + kpe
        r = lax.rsqrt(jnp.sum(t * t, axis=-1, keepdims=True) * inv_qk + EPS)
        k_ref[:, hd * HEAD_SLOT:(hd + 1) * HEAD_SLOT] = ((t * gck + rot_pe) * r + kpad).astype(BF16)

    vt = _bdot_nt(wvt_ref[...], ckv)
    ones = jnp.ones((VT_ROWS - MLA_V, vt.shape[1]), BF16)
    for hd in range(MLA_H):
        vt_ref[hd, 0:MLA_V, :] = vt[hd * MLA_V:(hd + 1) * MLA_V, :].astype(BF16)
        vt_ref[hd, MLA_V:, :] = ones


def _mla_project(xc, modtab, norm_g, w_in, g_qa, g_kva, w_qbt, w_kb, w_vt,
                 gcq, gsq, qpad, gck, gsk, kpad, n_ctx):
    nb, s, d = xc.shape
    rb = ROW_BLOCK
    ctx_blocks = n_ctx // rb
    qk_n = MLA_H * HEAD_SLOT
    qblk = lambda b, i: (b, 0, jnp.maximum(i - ctx_blocks, 0))
    return pl.pallas_call(
        functools.partial(_mla_proj_kernel, ctx_blocks=ctx_blocks),
        grid=(nb, s // rb),
        in_specs=[
            pl.BlockSpec((None, rb, d), lambda b, i: (b, i, 0)),
            pl.BlockSpec((None, None, 1, 6 * d),
                         lambda b, i: ((i >= ctx_blocks).astype(jnp.int32), b, 0, 0)),
            _const_spec((1, d)),
            _const_spec(w_in.shape),
            _const_spec(g_qa.shape),
            _const_spec(g_kva.shape),
            _const_spec(w_qbt.shape),
            _const_spec(w_kb.shape),
            _const_spec(w_vt.shape),
            pl.BlockSpec((MLA_QK, rb), lambda b, i: (0, jnp.maximum(i - ctx_blocks, 0))),
            pl.BlockSpec((MLA_QK, rb), lambda b, i: (0, jnp.maximum(i - ctx_blocks, 0))),
            _const_spec(qpad.shape),
            pl.BlockSpec((rb, HEAD_SLOT), lambda b, i: (i, 0)),
            pl.BlockSpec((rb, HEAD_SLOT), lambda b, i: (i, 0)),
            _const_spec(kpad.shape),
        ],
        out_specs=[
            pl.BlockSpec((None, qk_n, rb), qblk),
            pl.BlockSpec((None, rb, qk_n), lambda b, i: (b, i, 0)),
            pl.BlockSpec((None, MLA_H, VT_ROWS, rb), lambda b, i: (b, 0, 0, i)),
        ],
        out_shape=[
            jax.ShapeDtypeStruct((nb, qk_n, s - n_ctx), BF16),
            jax.ShapeDtypeStruct((nb, s, qk_n), BF16),
            jax.ShapeDtypeStruct((nb, MLA_H, VT_ROWS, s), BF16),
        ],
        compiler_params=pltpu.CompilerParams(
            dimension_semantics=("arbitrary", "arbitrary"), vmem_limit_bytes=VMEM_LIMIT),
        name="mla_project",
    )(xc, modtab, norm_g, w_in, g_qa, g_kva, w_qbt, w_kb, w_vt, gcq, gsq, qpad, gck, gsk, kpad)


def _attn_kernel(qt_ref, k_ref, vt_ref, o_ref, *, bounded):
    tq = ATTN_Q_BLOCK
    nkeys = k_ref.shape[0]
    bounds = list(range(0, nkeys, ATTN_KEY_CHUNK)) + [nkeys]
    nchunks = len(bounds) - 1
    zero = jnp.zeros((HEAD_SLOT, tq), BF16)

    for sub in range(ATTN_Q_STEP // tq):
        qs = slice(sub * tq, (sub + 1) * tq)
        qbd = jnp.concatenate(
            [jnp.concatenate([qt_ref[0:HEAD_SLOT, qs], zero], axis=1),
             jnp.concatenate([zero, qt_ref[HEAD_SLOT:, qs]], axis=1)], axis=0)

        def scores(c, qbd=qbd):
            return _bdot(k_ref[bounds[c]:bounds[c + 1], :], qbd)

        m = jnp.full((1, HEADS_PER_STEP * tq), -jnp.inf, F32)
        accs = [jnp.zeros((VT_ROWS, tq), F32) for _ in range(HEADS_PER_STEP)]
        s_next = scores(0)
        for c in range(nchunks):
            s = s_next
            if c + 1 < nchunks:
                s_next = scores(c + 1)
            if not bounded:
                m_new = jnp.maximum(m, jnp.max(s, axis=0, keepdims=True))
                alpha = jnp.exp2(m - m_new)
                s = s - m_new
                m = m_new
            pb = jnp.exp2(s).astype(BF16)
            for hd in range(HEADS_PER_STEP):
                pv = _bdot(vt_ref[hd, :, bounds[c]:bounds[c + 1]], pb[:, hd * tq:(hd + 1) * tq])
                if bounded:
                    accs[hd] = accs[hd] + pv
                else:
                    accs[hd] = accs[hd] * alpha[:, hd * tq:(hd + 1) * tq] + pv
        outs = [a[0:MLA_V, :] * (1.0 / a[MLA_V:MLA_V + 1, :]) for a in accs]
        o_ref[qs, :] = jnp.concatenate(outs, axis=0).T.astype(BF16)


def _attention(qt, k, vt, bounded):
    nb, _, seq = qt.shape
    s = k.shape[1]
    hp = MLA_H // HEADS_PER_STEP
    return pl.pallas_call(
        functools.partial(_attn_kernel, bounded=bounded),
        grid=(nb, hp, seq // ATTN_Q_STEP),
        in_specs=[
            pl.BlockSpec((None, HEADS_PER_STEP * HEAD_SLOT, ATTN_Q_STEP), lambda b, h, i: (b, h, i)),
            pl.BlockSpec((None, s, HEADS_PER_STEP * HEAD_SLOT), lambda b, h, i: (b, 0, h)),
            pl.BlockSpec((None, HEADS_PER_STEP, VT_ROWS, s), lambda b, h, i: (b, h, 0, 0)),
        ],
        out_specs=pl.BlockSpec((None, ATTN_Q_STEP, HEADS_PER_STEP * MLA_V), lambda b, h, i: (b, i, h)),
        out_shape=jax.ShapeDtypeStruct((nb, seq, MLA_H * MLA_V), BF16),
        compiler_params=pltpu.CompilerParams(
            dimension_semantics=("arbitrary", "arbitrary", "arbitrary"),
            vmem_limit_bytes=VMEM_LIMIT),
        name="mla_attention_bounded" if bounded else "mla_attention_online",
    )(qt, k, vt)


def _s5_discretize(A_re, A_im, log_step, B_re, B_im):
    lr = -jnp.abs(A_re)
    li = A_im
    dt = jnp.exp(log_step)[..., None]
    mag = jnp.exp(lr * dt)
    ar = mag * jnp.cos(li * dt)
    ai = mag * jnp.sin(li * dt)
    den = lr * lr + li * li
    cr = ((ar - 1) * lr + ai * li) / den
    ci = (ai * lr - (ar - 1) * li) / den
    bbr = cr[..., None] * B_re - ci[..., None] * B_im
    bbi = cr[..., None] * B_im + ci[..., None] * B_re
    return ar, ai, bbr, bbi


def _block_diag_tiles(w, inner):
    nt, rows, _ = w.shape
    gt = S5_GROUPS_PER_TILE
    tiled = jnp.tile(w, (1, 1, gt))
    rg = lax.broadcasted_iota(jnp.int32, (rows, gt * inner), 0) // (rows // gt)
    cg = lax.broadcasted_iota(jnp.int32, (rows, gt * inner), 1) // inner
    return jnp.where((rg == cg)[None], tiled, 0.0)


def _s5_pack(A_re, A_im, log_step, B_re, B_im, C_re, C_im):
    nt = S5_NTILES
    ar, ai, bbr, bbi = _s5_discretize(A_re, A_im, log_step, B_re, B_im)

    def tile_in(w):
        w = w.transpose(0, 1, 3, 2).reshape(2 * nt, S5_TILE_CH, S5_P)
        return _block_diag_tiles(w, S5_P)

    def tile_out(w):
        w = w.transpose(0, 1, 3, 2).reshape(2 * nt, S5_TILE_ST, S5_GC)
        return _block_diag_tiles(w, S5_GC)

    wb = jnp.concatenate([tile_in(bbr), tile_in(bbi)], axis=2).astype(BF16)
    wc = jnp.concatenate([tile_out(C_re), tile_out(-C_im)], axis=1).astype(BF16)
    av = jnp.stack([ar.reshape(2, -1), ai.reshape(2, -1)], axis=1)
    return (wb.reshape(2, nt, S5_TILE_CH, 2 * S5_TILE_ST), av,
            wc.reshape(2, nt, 2 * S5_TILE_ST, S5_TILE_CH))


def _chunk_permutation(nb, lc):
    r = np.arange(nb * lc)
    src = (r % nb) * lc + r // nb
    p = np.zeros((nb * lc, nb * lc), np.float32)
    p[r, src] = 1.0
    return p


def _head_slot_maps():
    full = np.full((HEAD_SLOT,), MLA_QK, np.int32)
    nope = np.full((HEAD_SLOT,), MLA_NOPE, np.int32)
    pe = np.full((HEAD_SLOT,), MLA_ROPE, np.int32)
    for dim in range(MLA_NOPE):
        lane = 16 + dim if dim < 48 else 32 + dim
        full[lane] = dim
        nope[lane] = dim
    half = MLA_ROPE // 4
    for e in range(MLA_ROPE):
        axis, hf, f = e // (2 * half), (e % (2 * half)) // half, e % half
        lane = (ROPE_PAIR if hf == 0 else 0) + axis * half + f
        full[lane] = MLA_NOPE + e
        pe[lane] = e
    return full, nope, pe


def _take_padded(w, idx):
    zero = jnp.zeros(w.shape[:-1] + (1,), w.dtype)
    return jnp.take(jnp.concatenate([w, zero], axis=-1), jnp.asarray(idx), axis=-1)


def _rope_tables(n_ctx, seq):
    half = MLA_ROPE // 4
    rows = seq // GRID_W
    row = jnp.repeat(jnp.arange(rows, dtype=F32), GRID_W)
    col = jnp.tile(jnp.arange(GRID_W, dtype=F32), rows)
    axis_dim = MLA_ROPE // 2
    inv_freq = ROPE_BASE ** (-jnp.arange(0, axis_dim, 2, dtype=F32) / axis_dim)
    ang = jnp.stack([row[:, None] * inv_freq, col[:, None] * inv_freq], axis=1).reshape(seq, 2 * half)
    c, s = jnp.cos(ang), jnp.sin(ang)
    cos = jnp.ones((seq, HEAD_SLOT), F32)
    sin = jnp.zeros((seq, HEAD_SLOT), F32)
    cos = cos.at[:, 0:2 * half].set(c).at[:, ROPE_PAIR:ROPE_PAIR + 2 * half].set(c)
    sin = sin.at[:, 0:2 * half].set(s).at[:, ROPE_PAIR:ROPE_PAIR + 2 * half].set(-s)
    cos = jnp.concatenate([jnp.ones((n_ctx, HEAD_SLOT), F32), cos], axis=0)
    sin = jnp.concatenate([jnp.zeros((n_ctx, HEAD_SLOT), F32), sin], axis=0)
    return cos, sin


def kernel(x, c, ctx, c_ctx, mod_w, mod_b, norm_mix, norm_ffn, ffn_w_in, ffn_w_out, s5_w_in, s5_A_re, s5_A_im, s5_log_step, s5_B_re, s5_B_im, s5_C_re, s5_C_im, s5_D, s5_w_glu, s5_w_out, mla_w_in, mla_q_a_norm, mla_kv_a_norm, mla_w_q_b, mla_w_kv_b, mla_q_norm, mla_k_norm, mla_w_o):
    nb, seq, d = x.shape
    n_ctx = ctx.shape[1]
    assert d == D_MODEL and nb == SUBLANES
    assert n_ctx % ROW_BLOCK == 0 and seq % ATTN_Q_STEP == 0 and seq % GRID_W == 0

    cvec = jnp.concatenate([c, c_ctx[None, :], jnp.zeros((16 - nb - 1, d), F32)], axis=0)
    mod = _modulation(cvec, mod_w, mod_b)
    mod_x = mod[:, :nb, None, :]
    mod_c = jnp.broadcast_to(mod[:, nb:nb + 1, None, :], mod_x.shape)
    modtab = jnp.stack([mod_c, mod_x], axis=1)

    perm = _chunk_permutation(nb, S5_CHUNK_T)
    wb, av, wc = _s5_pack(s5_A_re[0], s5_A_im[0], s5_log_step[0], s5_B_re[0], s5_B_im[0],
                          s5_C_re[0], s5_C_im[0])
    gl = _s5_mixer(x, ctx, modtab[0], norm_mix[0][None, :], s5_w_in[0].astype(BF16),
                   jnp.asarray(perm, BF16), jnp.asarray(perm.T, BF16), wb, av, wc, s5_D[0][None, :])
    w1 = ffn_w_in.astype(BF16)
    w2 = ffn_w_out.astype(BF16)
    xc = _s5_tail(x, ctx, gl, modtab[0], norm_ffn[0][None, :],
                  s5_w_glu[0].astype(BF16), s5_w_out[0].astype(BF16), w1[0], w2[0])

    full_idx, nope_idx, pe_idx = _head_slot_maps()
    w_in = mla_w_in[0]
    w_in_slots = jnp.concatenate(
        [w_in[:, :MLA_QL + MLA_KVL], _take_padded(w_in[:, MLA_QL + MLA_KVL:], pe_idx)], axis=1)
    w_qbt = _take_padded(mla_w_q_b[0].reshape(MLA_QL, MLA_H, MLA_QK), full_idx)
    w_qbt = w_qbt.reshape(MLA_QL, MLA_H * HEAD_SLOT).T
    w_kvb = mla_w_kv_b[0].reshape(MLA_KVL, MLA_H, MLA_NOPE + MLA_V)
    w_kb = _take_padded(w_kvb[:, :, :MLA_NOPE], nope_idx).reshape(MLA_KVL, MLA_H * HEAD_SLOT)
    w_vt = w_kvb[:, :, MLA_NOPE:].reshape(MLA_KVL, MLA_H * MLA_V).T
    cos, sin = _rope_tables(n_ctx, seq)
    q_scale = (MLA_QK ** -0.5) * math.log2(math.e)
    g_q = _take_padded(mla_q_norm[0], full_idx)
    g_k = _take_padded(mla_k_norm[0], full_idx)
    gcq = ((g_q * q_scale) * cos[n_ctx:, :])[:, :MLA_QK].T
    gsq = ((jnp.roll(g_q, ROPE_PAIR) * q_scale) * sin[n_ctx:, :])[:, :MLA_QK].T
    gck = g_k * cos
    gsk = jnp.roll(g_k, ROPE_PAIR) * sin
    score_bound = 1.02 * MLA_QK * q_scale * jnp.max(jnp.abs(mla_q_norm[0])) * jnp.max(jnp.abs(mla_k_norm[0]))
    qpad = jnp.zeros((SUBLANES, ROW_BLOCK), F32).at[0, :].set(-score_bound)
    kpad = jnp.zeros((1, HEAD_SLOT), F32).at[0, PAD_LANE].set(1.0)
    qt, k, vt = _mla_project(
        xc, modtab[1], norm_mix[1][None, :], w_in_slots.astype(BF16),
        mla_q_a_norm[0][None, :], mla_kv_a_norm[0][None, :],
        w_qbt.astype(BF16), w_kb.astype(BF16), w_vt.astype(BF16),
        gcq, gsq, qpad, gck, gsk, kpad, n_ctx)
    att = lax.cond(score_bound <= SCORE_BOUND_LIMIT,
                   lambda: _attention(qt, k, vt, True),
                   lambda: _attention(qt, k, vt, False))
    return _mla_tail(xc, att, modtab[1], norm_ffn[1][None, :], mla_w_o[0].astype(BF16), w1[1], w2[1],
                     n_ctx)
```

```python
import functools
import math

import jax
import jax.numpy as jnp
import numpy as np
from jax import lax
from jax.experimental import pallas as pl
from jax.experimental.pallas import tpu as pltpu

F32 = jnp.float32
BF16 = jnp.bfloat16

D_MODEL = 1024
EPS = 1e-6
S5_GC = 16
S5_G = D_MODEL // S5_GC
S5_P = 64
S5_STATE = S5_G * S5_P
S5_GROUPS_PER_TILE = 16
S5_TILE_CH = S5_GROUPS_PER_TILE * S5_GC
S5_TILE_ST = S5_GROUPS_PER_TILE * S5_P
S5_NTILES = S5_G // S5_GROUPS_PER_TILE
S5_CHUNK_T = 32
MLA_H = 16
MLA_NOPE = 64
MLA_ROPE = 32
MLA_V = 64
MLA_QL = D_MODEL // 2
MLA_KVL = D_MODEL // 4
MLA_QK = MLA_NOPE + MLA_ROPE
HEAD_SLOT = 128
ROPE_PAIR = HEAD_SLOT // 2
ROPE_HALF = MLA_ROPE // 2
PAD_LANE = MLA_QK
SUBLANES = 8
GRID_W = 64
ROPE_BASE = 10000.0
FFN_HIDDEN = 2816
FFN_CHUNK = 256
ROW_BLOCK = 256
ATTN_Q_BLOCK = 256
ATTN_Q_STEP = 2 * ATTN_Q_BLOCK
ATTN_KEY_CHUNK = 1024
HEADS_PER_STEP = 2
VT_ROWS = MLA_V + 16
SCORE_BOUND_LIMIT = 50.0

VMEM_LIMIT = 56 * 1024 * 1024


def _const_spec(shape):
    nd = len(shape)
    return pl.BlockSpec(shape, lambda *_: (0,) * nd, pipeline_mode=pl.Buffered(1))


def _layer_spec(shape, layer):
    nd = len(shape) - 1
    return pl.BlockSpec((None,) + tuple(shape[1:]), lambda *_: (layer,) + (0,) * nd,
                        pipeline_mode=pl.Buffered(1))


def _rms(x, g):
    ms = jnp.mean(x * x, axis=-1, keepdims=True)
    return x * lax.rsqrt(ms + EPS) * g


def _modnorm(x, g, shift, scale):
    return _rms(x, g) * (1.0 + scale) + shift


def _bdot(a, b):
    return jnp.dot(a, b, preferred_element_type=F32)


def _bdot_nt(a, b):
    return lax.dot_general(a, b, (((1,), (1,)), ((), ())), preferred_element_type=F32)


def _silu(x):
    return x * jax.nn.sigmoid(x)


def _gelu_tanh(x):
    c = math.sqrt(2.0 / math.pi)
    return 0.5 * x * (1.0 + jnp.tanh(c * (x + 0.044715 * (x * x * x))))


def _mod_kernel(c_ref, w_ref, b_ref, o_ref):
    a = _silu(c_ref[...]).astype(BF16)
    o_ref[...] = _bdot(a, w_ref[...].astype(BF16)) + b_ref[...]


def _modulation(cvec, mod_w, mod_b):
    depth, d, n = mod_w.shape
    nb = 1536
    return pl.pallas_call(
        _mod_kernel,
        grid=(depth, n // nb),
        in_specs=[
            pl.BlockSpec((16, d), lambda i, j: (0, 0)),
            pl.BlockSpec((None, d, nb), lambda i, j: (i, 0, j)),
            pl.BlockSpec((None, 1, nb), lambda i, j: (i, 0, j)),
        ],
        out_specs=pl.BlockSpec((None, 16, nb), lambda i, j: (i, 0, j)),
        out_shape=jax.ShapeDtypeStruct((depth, 16, n), F32),
        compiler_params=pltpu.CompilerParams(vmem_limit_bytes=VMEM_LIMIT),
        name="modulation",
    )(cvec, mod_w, mod_b.reshape(depth, 1, n))


def _s5_scan_chunk(is_ctx, x_ref, ctx_ref, mod_ref, g_ref, win_ref, perm_ref, wb_ref, a_ref, wc_ref,
                   bbuf, hstate, *, reverse):
    nb, lc, d = x_ref.shape
    rows = nb * lc

    @pl.when(pl.program_id(0) == 0)
    def _():
        hstate[...] = jnp.zeros_like(hstate)

    xin = jnp.where(is_ctx, ctx_ref[...], x_ref[...])
    shift = mod_ref[:, :, 0:d]
    scale = mod_ref[:, :, d:2 * d]
    h = _modnorm(xin, g_ref[...], shift, scale)
    hb = h.reshape(rows, d).astype(BF16)
    htb = _bdot(perm_ref[...], hb).astype(BF16)
    u = _bdot(htb, win_ref[...])
    ub = u.astype(BF16)
    def expand(j):
        bj = _bdot(ub[:, j * S5_TILE_CH:(j + 1) * S5_TILE_CH], wb_ref[j])
        bbuf[:, j * S5_TILE_ST:(j + 1) * S5_TILE_ST] = bj[:, :S5_TILE_ST]
        bbuf[:, S5_STATE + j * S5_TILE_ST:S5_STATE + (j + 1) * S5_TILE_ST] = bj[:, S5_TILE_ST:]

    ys = []
    expand(0)
    for j in range(S5_NTILES):
        if j + 1 < S5_NTILES:
            expand(j + 1)
        cr = slice(j * S5_TILE_ST, (j + 1) * S5_TILE_ST)
        ci = slice(S5_STATE + j * S5_TILE_ST, S5_STATE + (j + 1) * S5_TILE_ST)
        ar = jnp.broadcast_to(a_ref[0:1, cr], (nb, S5_TILE_ST))
        ai = jnp.broadcast_to(a_ref[1:2, cr], (nb, S5_TILE_ST))
        hr, hi = hstate[:, cr], hstate[:, ci]
        for i in range(lc):
            t = (lc - 1 - i) if reverse else i
            r = slice(t * nb, (t + 1) * nb)
            hr, hi = (ar * hr - ai * hi) + bbuf[r, cr], (ar * hi + ai * hr) + bbuf[r, ci]
            bbuf[r, cr] = hr
            bbuf[r, ci] = hi
        hstate[:, cr] = hr
        hstate[:, ci] = hi
        ys.append(_bdot(bbuf[:, cr].astype(BF16), wc_ref[j, 0:S5_TILE_ST, :])
                  + _bdot(bbuf[:, ci].astype(BF16), wc_ref[j, S5_TILE_ST:, :]))
    return u, jnp.concatenate(ys, axis=1)


def _s5_fwd_kernel(x_ref, ctx_ref, mod_ref, g_ref, win_ref, perm_ref, wb_ref, a_ref, wc_ref,
                   yf_ref, bbuf, hstate, *, ctx_chunks):
    is_ctx = pl.program_id(0) < ctx_chunks
    _, y = _s5_scan_chunk(is_ctx, x_ref, ctx_ref, mod_ref, g_ref, win_ref, perm_ref, wb_ref, a_ref,
                          wc_ref, bbuf, hstate, reverse=False)
    yf_ref[...] = y


def _s5_bwd_kernel(x_ref, ctx_ref, mod_ref, g_ref, win_ref, perm_ref, wb_ref, a_ref, wc_ref,
                   yf_ref, dskip_ref, permt_ref, o_ref, bbuf, hstate, *, ctx_chunks):
    is_ctx = pl.program_id(0) < ctx_chunks
    u, y = _s5_scan_chunk(is_ctx, x_ref, ctx_ref, mod_ref, g_ref, win_ref, perm_ref, wb_ref, a_ref,
                          wc_ref, bbuf, hstate, reverse=True)
    y = y + yf_ref[...] + dskip_ref[...] * u
    gb = _gelu_tanh(y).astype(BF16)
    o_ref[...] = _bdot(permt_ref[...], gb).astype(BF16).reshape(o_ref.shape)


def _s5_mixer(x, ctx, modtab, norm_g, w_in, perm, permt, wb, a, wc, dskip):
    nb, seq, d = x.shape
    n_ctx = ctx.shape[1]
    lc = S5_CHUNK_T
    rows = nb * lc
    ctx_chunks = n_ctx // lc
    nchunks = ctx_chunks + seq // lc

    def fwd_chunk(i):
        return i

    def bwd_chunk(i):
        return jnp.where(i < ctx_chunks, ctx_chunks - 1 - i, nchunks - 1 + ctx_chunks - i)

    def specs(chunk_of):
        return [
            pl.BlockSpec((nb, lc, d), lambda i: (0, jnp.maximum(chunk_of(i) - ctx_chunks, 0), 0)),
            pl.BlockSpec((nb, lc, d), lambda i: (0, jnp.minimum(chunk_of(i), ctx_chunks - 1), 0)),
            pl.BlockSpec((None, nb, 1, 6 * d),
                         lambda i: ((chunk_of(i) >= ctx_chunks).astype(jnp.int32), 0, 0, 0)),
            _const_spec((1, d)),
            _const_spec((d, d)),
            _const_spec((rows, rows)),
        ]

    def dir_specs(k):
        return [
            pl.BlockSpec((None, S5_NTILES, S5_TILE_CH, 2 * S5_TILE_ST), lambda i: (k, 0, 0, 0),
                         pipeline_mode=pl.Buffered(1)),
            pl.BlockSpec((None, 2, S5_STATE), lambda i: (k, 0, 0), pipeline_mode=pl.Buffered(1)),
            pl.BlockSpec((None, S5_NTILES, 2 * S5_TILE_ST, S5_TILE_CH), lambda i: (k, 0, 0, 0),
                         pipeline_mode=pl.Buffered(1)),
        ]

    scratch = [pltpu.VMEM((rows, 2 * S5_STATE), F32), pltpu.VMEM((nb, 2 * S5_STATE), F32)]
    params = pltpu.CompilerParams(dimension_semantics=("arbitrary",), vmem_limit_bytes=VMEM_LIMIT)

    yf = pl.pallas_call(
        functools.partial(_s5_fwd_kernel, ctx_chunks=ctx_chunks),
        grid=(nchunks,),
        in_specs=specs(fwd_chunk) + dir_specs(0),
        out_specs=pl.BlockSpec((None, rows, d), lambda i: (i, 0, 0)),
        out_shape=jax.ShapeDtypeStruct((nchunks, rows, d), F32),
        scratch_shapes=scratch,
        compiler_params=params,
        name="s5_forward",
    )(x, ctx, modtab, norm_g, w_in, perm, wb, a, wc)

    return pl.pallas_call(
        functools.partial(_s5_bwd_kernel, ctx_chunks=ctx_chunks),
        grid=(nchunks,),
        in_specs=specs(bwd_chunk) + dir_specs(1) + [
            pl.BlockSpec((None, rows, d), lambda i: (bwd_chunk(i), 0, 0)),
            _const_spec((1, d)),
            _const_spec((rows, rows)),
        ],
        out_specs=pl.BlockSpec((nb, lc, d), lambda i: (0, bwd_chunk(i), 0)),
        out_shape=jax.ShapeDtypeStruct((nb, n_ctx + seq, d), BF16),
        scratch_shapes=scratch,
        compiler_params=params,
        name="s5_backward",
    )(x, ctx, modtab, norm_g, w_in, perm, wb, a, wc, yf, dskip, permt)


def _ffn_residual(x1, mod_ref, g_ref, w1_ref, w2_ref):
    d = x1.shape[-1]
    hn = _modnorm(x1, g_ref[...], mod_ref[:, 3 * d:4 * d], mod_ref[:, 4 * d:5 * d]).astype(BF16)

    def up(k):
        lo = k * FFN_CHUNK
        return (_bdot(hn, w1_ref[:, lo:lo + FFN_CHUNK]),
                _bdot(hn, w1_ref[:, FFN_HIDDEN + lo:FFN_HIDDEN + lo + FFN_CHUNK]))

    nchunks = FFN_HIDDEN // FFN_CHUNK
    acc = jnp.zeros(x1.shape, F32)
    nxt = up(0)
    for k in range(nchunks):
        gk, uk = nxt
        if k + 1 < nchunks:
            nxt = up(k + 1)
        acc = acc + _bdot((_silu(gk) * uk).astype(BF16), w2_ref[k * FFN_CHUNK:(k + 1) * FFN_CHUNK, :])
    return x1 + mod_ref[:, 5 * d:6 * d] * acc


def _s5_tail_kernel(x_ref, ctx_ref, gl_ref, mod_ref, gf_ref, wglu_ref, wout_ref, w1_ref, w2_ref,
                    o_ref, *, ctx_blocks):
    d = x_ref.shape[-1]
    xin = jnp.where(pl.program_id(1) < ctx_blocks, ctx_ref[...], x_ref[...])
    z = _bdot(gl_ref[...], wglu_ref[...])
    gated = (z[:, :d] * jax.nn.sigmoid(z[:, d:])).astype(BF16)
    x1 = xin + mod_ref[:, 2 * d:3 * d] * _bdot(gated, wout_ref[...])
    o_ref[...] = _ffn_residual(x1, mod_ref, gf_ref, w1_ref, w2_ref)


def _mla_tail_kernel(x_ref, at_ref, mod_ref, gf_ref, wo_ref, w1_ref, w2_ref, o_ref):
    d = x_ref.shape[-1]
    x1 = x_ref[...] + mod_ref[:, 2 * d:3 * d] * _bdot(at_ref[...], wo_ref[...])
    o_ref[...] = _ffn_residual(x1, mod_ref, gf_ref, w1_ref, w2_ref)


_TAIL_PARAMS = pltpu.CompilerParams(
    dimension_semantics=("arbitrary", "arbitrary"), vmem_limit_bytes=VMEM_LIMIT)


def _s5_tail(x, ctx, gl, modtab, norm_g, w_glu, w_out, w1, w2):
    nb, seq, d = x.shape
    n_ctx = ctx.shape[1]
    rb = ROW_BLOCK
    ctx_blocks = n_ctx // rb
    nblk = ctx_blocks + seq // rb
    tok = pl.BlockSpec((None, rb, d), lambda b, i: (b, i, 0))
    return pl.pallas_call(
        functools.partial(_s5_tail_kernel, ctx_blocks=ctx_blocks),
        grid=(nb, nblk),
        in_specs=[
            pl.BlockSpec((None, rb, d), lambda b, i: (b, jnp.maximum(i - ctx_blocks, 0), 0)),
            pl.BlockSpec((None, rb, d), lambda b, i: (b, jnp.minimum(i, ctx_blocks - 1), 0)),
            tok,
            pl.BlockSpec((None, None, 1, 6 * d),
                         lambda b, i: ((i >= ctx_blocks).astype(jnp.int32), b, 0, 0)),
            _const_spec((1, d)),
            _const_spec(w_glu.shape), _const_spec(w_out.shape),
            _layer_spec(w1.shape, 0), _layer_spec(w2.shape, 0),
        ],
        out_specs=tok,
        out_shape=jax.ShapeDtypeStruct((nb, nblk * rb, d), F32),
        compiler_params=_TAIL_PARAMS,
        name="s5_tail",
    )(x, ctx, gl, modtab, norm_g, w_glu, w_out, w1, w2)


def _mla_tail(xc, att, modtab, norm_g, w_o, w1, w2, n_ctx):
    nb, s, d = xc.shape
    rb = ROW_BLOCK
    off = n_ctx // rb
    nblk = s // rb - off
    tok = pl.BlockSpec((None, rb, d), lambda b, i: (b, i, 0))
    return pl.pallas_call(
        _mla_tail_kernel,
        grid=(nb, nblk),
        in_specs=[
            pl.BlockSpec((None, rb, d), lambda b, i: (b, i + off, 0)),
            tok,
            pl.BlockSpec((None, None, 1, 6 * d), lambda b, i: (1, b, 0, 0)),
            _const_spec((1, d)),
            _const_spec(w_o.shape), _layer_spec(w1.shape, 1), _layer_spec(w2.shape, 1),
        ],
        out_specs=tok,
        out_shape=jax.ShapeDtypeStruct((nb, nblk * rb, d), F32),
        compiler_params=_TAIL_PARAMS,
        name="mla_tail",
    )(xc, att, modtab, norm_g, w_o, w1, w2)


def _mla_proj_kernel(x_ref, mod_ref, g_ref, win_ref, gqa_ref, gkva_ref, wqbt_ref, wkb_ref, wvt_ref,
                     gcq_ref, gsq_ref, qpad_ref, gck_ref, gsk_ref, kpad_ref,
                     qt_ref, k_ref, vt_ref, *, ctx_blocks):
    d = x_ref.shape[-1]
    h = _modnorm(x_ref[...], g_ref[...], mod_ref[:, 0:d], mod_ref[:, d:2 * d]).astype(BF16)
    proj = _bdot(h, win_ref[...])
    inv_qk = 1.0 / MLA_QK
    lo, hi = slice(0, ROPE_HALF), slice(ROPE_PAIR, ROPE_PAIR + ROPE_HALF)

    @pl.when(pl.program_id(1) >= ctx_blocks)
    def _():
        ql = _rms(proj[:, 0:MLA_QL], gqa_ref[...]).astype(BF16)
        qt = _bdot_nt(wqbt_ref[...], ql)
        gc, gs = gcq_ref[...], gsq_ref[...]
        pad = jnp.zeros((HEAD_SLOT - MLA_QK - SUBLANES, qt.shape[1]), F32)
        for hd in range(MLA_H):
            seg = qt[hd * HEAD_SLOT:hd * HEAD_SLOT + MLA_QK, :]
            r = lax.rsqrt(jnp.sum(seg * seg, axis=0, keepdims=True) * inv_qk + EPS)
            rot = seg * gc
            rot = jnp.concatenate([rot[lo] + seg[hi] * gs[lo], rot[ROPE_HALF:ROPE_PAIR],
                                   rot[hi] + seg[lo] * gs[hi], rot[ROPE_PAIR + ROPE_HALF:]], axis=0)
            out = jnp.concatenate([rot * r, qpad_ref[...], pad], axis=0)
            qt_ref[hd * HEAD_SLOT:(hd + 1) * HEAD_SLOT, :] = out.astype(BF16)

    ckv = _rms(proj[:, MLA_QL:MLA_QL + MLA_KVL], gkva_ref[...]).astype(BF16)
    knope = _bdot(ckv, wkb_ref[...])
    kpe = proj[:, MLA_QL + MLA_KVL:]
    gck = gck_ref[...]
    rot_pe = pltpu.roll(kpe, ROPE_PAIR, 1) * gsk_ref[...]
    kpad = kpad_ref[...]
    for hd in range(MLA_H):
        t = knope[:, hd * HEAD_SLOT:(hd + 1) * HEAD_SLOT] + kpe
        r = lax.rsqrt(jnp.sum(t * t, axis=-1, keepdims=True) * inv_qk + EPS)
        k_ref[:, hd * HEAD_SLOT:(hd + 1) * HEAD_SLOT] = ((t * gck + rot_pe) * r + kpad).astype(BF16)

    vt = _bdot_nt(wvt_ref[...], ckv)
    ones = jnp.ones((VT_ROWS - MLA_V, vt.shape[1]), BF16)
    for hd in range(MLA_H):
        vt_ref[hd, 0:MLA_V, :] = vt[hd * MLA_V:(hd + 1) * MLA_V, :].astype(BF16)
        vt_ref[hd, MLA_V:, :] = ones


def _mla_project(xc, modtab, norm_g, w_in, g_qa, g_kva, w_qbt, w_kb, w_vt,
                 gcq, gsq, qpad, gck, gsk, kpad, n_ctx):
    nb, s, d = xc.shape
    rb = ROW_BLOCK
    ctx_blocks = n_ctx // rb
    qk_n = MLA_H * HEAD_SLOT
    qblk = lambda b, i: (b, 0, jnp.maximum(i - ctx_blocks, 0))
    return pl.pallas_call(
        functools.partial(_mla_proj_kernel, ctx_blocks=ctx_blocks),
        grid=(nb, s // rb),
        in_specs=[
            pl.BlockSpec((None, rb, d), lambda b, i: (b, i, 0)),
            pl.BlockSpec((None, None, 1, 6 * d),
                         lambda b, i: ((i >= ctx_blocks).astype(jnp.int32), b, 0, 0)),
            _const_spec((1, d)),
            _const_spec(w_in.shape),
            _const_spec(g_qa.shape),
            _const_spec(g_kva.shape),
            _const_spec(w_qbt.shape),
            _const_spec(w_kb.shape),
            _const_spec(w_vt.shape),
            pl.BlockSpec((MLA_QK, rb), lambda b, i: (0, jnp.maximum(i - ctx_blocks, 0))),
            pl.BlockSpec((MLA_QK, rb), lambda b, i: (0, jnp.maximum(i - ctx_blocks, 0))),
            _const_spec(qpad.shape),
            pl.BlockSpec((rb, HEAD_SLOT), lambda b, i: (i, 0)),
            pl.BlockSpec((rb, HEAD_SLOT), lambda b, i: (i, 0)),
            _const_spec(kpad.shape),
        ],
        out_specs=[
            pl.BlockSpec((None, qk_n, rb), qblk),
            pl.BlockSpec((None, rb, qk_n), lambda b, i: (b, i, 0)),
            pl.BlockSpec((None, MLA_H, VT_ROWS, rb), lambda b, i: (b, 0, 0, i)),
        ],
        out_shape=[
            jax.ShapeDtypeStruct((nb, qk_n, s - n_ctx), BF16),
            jax.ShapeDtypeStruct((nb, s, qk_n), BF16),
            jax.ShapeDtypeStruct((nb, MLA_H, VT_ROWS, s), BF16),
        ],
        compiler_params=pltpu.CompilerParams(
            dimension_semantics=("arbitrary", "arbitrary"), vmem_limit_bytes=VMEM_LIMIT),
        name="mla_project",
    )(xc, modtab, norm_g, w_in, g_qa, g_kva, w_qbt, w_kb, w_vt, gcq, gsq, qpad, gck, gsk, kpad)


def _attn_kernel(qt_ref, k_ref, vt_ref, o_ref, *, bounded):
    tq = ATTN_Q_BLOCK
    nkeys = k_ref.shape[0]
    bounds = list(range(0, nkeys, ATTN_KEY_CHUNK)) + [nkeys]
    nchunks = len(bounds) - 1
    zero = jnp.zeros((HEAD_SLOT, tq), BF16)

    for sub in range(ATTN_Q_STEP // tq):
        qs = slice(sub * tq, (sub + 1) * tq)
        qbd = jnp.concatenate(
            [jnp.concatenate([qt_ref[0:HEAD_SLOT, qs], zero], axis=1),
             jnp.concatenate([zero, qt_ref[HEAD_SLOT:, qs]], axis=1)], axis=0)

        def scores(c, qbd=qbd):
            return _bdot(k_ref[bounds[c]:bounds[c + 1], :], qbd)

        m = jnp.full((1, HEADS_PER_STEP * tq), -jnp.inf, F32)
        accs = [jnp.zeros((VT_ROWS, tq), F32) for _ in range(HEADS_PER_STEP)]
        s_next = scores(0)
        for c in range(nchunks):
            s = s_next
            if c + 1 < nchunks:
                s_next = scores(c + 1)
            if not bounded:
                m_new = jnp.maximum(m, jnp.max(s, axis=0, keepdims=True))
                alpha = jnp.exp2(m - m_new)
                s = s - m_new
                m = m_new
            pb = jnp.exp2(s).astype(BF16)
            for hd in range(HEADS_PER_STEP):
                pv = _bdot(vt_ref[hd, :, bounds[c]:bounds[c + 1]], pb[:, hd * tq:(hd + 1) * tq])
                if bounded:
                    accs[hd] = accs[hd] + pv
                else:
                    accs[hd] = accs[hd] * alpha[:, hd * tq:(hd + 1) * tq] + pv
        outs = [a[0:MLA_V, :] * (1.0 / a[MLA_V:MLA_V + 1, :]) for a in accs]
        o_ref[qs, :] = jnp.concatenate(outs, axis=0).T.astype(BF16)


def _attention(qt, k, vt, bounded):
    nb, _, seq = qt.shape
    s = k.shape[1]
    hp = MLA_H // HEADS_PER_STEP
    return pl.pallas_call(
        functools.partial(_attn_kernel, bounded=bounded),
        grid=(nb, hp, seq // ATTN_Q_STEP),
        in_specs=[
            pl.BlockSpec((None, HEADS_PER_STEP * HEAD_SLOT, ATTN_Q_STEP), lambda b, h, i: (b, h, i)),
            pl.BlockSpec((None, s, HEADS_PER_STEP * HEAD_SLOT), lambda b, h, i: (b, 0, h)),
            pl.BlockSpec((None, HEADS_PER_STEP, VT_ROWS, s), lambda b, h, i: (b, h, 0, 0)),
        ],
        out_specs=pl.BlockSpec((None, ATTN_Q_STEP, HEADS_PER_STEP * MLA_V), lambda b, h, i: (b, i, h)),
        out_shape=jax.ShapeDtypeStruct((nb, seq, MLA_H * MLA_V), BF16),
        compiler_params=pltpu.CompilerParams(
            dimension_semantics=("arbitrary", "arbitrary", "arbitrary"),
            vmem_limit_bytes=VMEM_LIMIT),
        name="mla_attention_bounded" if bounded else "mla_attention_online",
    )(qt, k, vt)


def _s5_discretize(A_re, A_im, log_step, B_re, B_im):
    lr = -jnp.abs(A_re)
    li = A_im
    dt = jnp.exp(log_step)[..., None]
    mag = jnp.exp(lr * dt)
    ar = mag * jnp.cos(li * dt)
    ai = mag * jnp.sin(li * dt)
    den = lr * lr + li * li
    cr = ((ar - 1) * lr + ai * li) / den
    ci = (ai * lr - (ar - 1) * li) / den
    bbr = cr[..., None] * B_re - ci[..., None] * B_im
    bbi = cr[..., None] * B_im + ci[..., None] * B_re
    return ar, ai, bbr, bbi


def _block_diag_tiles(w, inner):
    nt, rows, _ = w.shape
    gt = S5_GROUPS_PER_TILE
    tiled = jnp.tile(w, (1, 1, gt))
    rg = lax.broadcasted_iota(jnp.int32, (rows, gt * inner), 0) // (rows // gt)
    cg = lax.broadcasted_iota(jnp.int32, (rows, gt * inner), 1) // inner
    return jnp.where((rg == cg)[None], tiled, 0.0)


def _s5_pack(A_re, A_im, log_step, B_re, B_im, C_re, C_im):
    nt = S5_NTILES
    ar, ai, bbr, bbi = _s5_discretize(A_re, A_im, log_step, B_re, B_im)

    def tile_in(w):
        w = w.transpose(0, 1, 3, 2).reshape(2 * nt, S5_TILE_CH, S5_P)
        return _block_diag_tiles(w, S5_P)

    def tile_out(w):
        w = w.transpose(0, 1, 3, 2).reshape(2 * nt, S5_TILE_ST, S5_GC)
        return _block_diag_tiles(w, S5_GC)

    wb = jnp.concatenate([tile_in(bbr), tile_in(bbi)], axis=2).astype(BF16)
    wc = jnp.concatenate([tile_out(C_re), tile_out(-C_im)], axis=1).astype(BF16)
    av = jnp.stack([ar.reshape(2, -1), ai.reshape(2, -1)], axis=1)
    return (wb.reshape(2, nt, S5_TILE_CH, 2 * S5_TILE_ST), av,
            wc.reshape(2, nt, 2 * S5_TILE_ST, S5_TILE_CH))


def _chunk_permutation(nb, lc):
    r = np.arange(nb * lc)
    src = (r % nb) * lc + r // nb
    p = np.zeros((nb * lc, nb * lc), np.float32)
    p[r, src] = 1.0
    return p


def _head_slot_maps():
    full = np.full((HEAD_SLOT,), MLA_QK, np.int32)
    nope = np.full((HEAD_SLOT,), MLA_NOPE, np.int32)
    pe = np.full((HEAD_SLOT,), MLA_ROPE, np.int32)
    for dim in range(MLA_NOPE):
        lane = 16 + dim if dim < 48 else 32 + dim
        full[lane] = dim
        nope[lane] = dim
    half = MLA_ROPE // 4
    for e in range(MLA_ROPE):
        axis, hf, f = e // (2 * half), (e % (2 * half)) // half, e % half
        lane = (ROPE_PAIR if hf == 0 else 0) + axis * half + f
        full[lane] = MLA_NOPE + e
        pe[lane] = e
    return full, nope, pe


def _take_padded(w, idx):
    zero = jnp.zeros(w.shape[:-1] + (1,), w.dtype)
    return jnp.take(jnp.concatenate([w, zero], axis=-1), jnp.asarray(idx), axis=-1)


def _rope_tables(n_ctx, seq):
    half = MLA_ROPE // 4
    rows = seq // GRID_W
    row = jnp.repeat(jnp.arange(rows, dtype=F32), GRID_W)
    col = jnp.tile(jnp.arange(GRID_W, dtype=F32), rows)
    axis_dim = MLA_ROPE // 2
    inv_freq = ROPE_BASE ** (-jnp.arange(0, axis_dim, 2, dtype=F32) / axis_dim)
    ang = jnp.stack([row[:, None] * inv_freq, col[:, None] * inv_freq], axis=1).reshape(seq, 2 * half)
    c, s = jnp.cos(ang), jnp.sin(ang)
    one = jnp.ones((seq, ROPE_PAIR - 2 * half), F32)
    cos = jnp.concatenate([c, one, c, one], axis=1)
    sin = jnp.concatenate([s, 0.0 * one, -s, 0.0 * one], axis=1)
    cos = jnp.concatenate([jnp.ones((n_ctx, HEAD_SLOT), F32), cos], axis=0)
    sin = jnp.concatenate([jnp.zeros((n_ctx, HEAD_SLOT), F32), sin], axis=0)
    return cos, sin


def kernel(x, c, ctx, c_ctx, mod_w, mod_b, norm_mix, norm_ffn, ffn_w_in, ffn_w_out, s5_w_in, s5_A_re, s5_A_im, s5_log_step, s5_B_re, s5_B_im, s5_C_re, s5_C_im, s5_D, s5_w_glu, s5_w_out, mla_w_in, mla_q_a_norm, mla_kv_a_norm, mla_w_q_b, mla_w_kv_b, mla_q_norm, mla_k_norm, mla_w_o):
    nb, seq, d = x.shape
    n_ctx = ctx.shape[1]
    assert d == D_MODEL and nb == SUBLANES
    assert n_ctx % ROW_BLOCK == 0 and seq % ATTN_Q_STEP == 0 and seq % GRID_W == 0

    cvec = jnp.concatenate([c, c_ctx[None, :], jnp.zeros((16 - nb - 1, d), F32)], axis=0)
    mod = _modulation(cvec, mod_w, mod_b)
    mod_x = mod[:, :nb, None, :]
    mod_c = jnp.broadcast_to(mod[:, nb:nb + 1, None, :], mod_x.shape)
    modtab = jnp.stack([mod_c, mod_x], axis=1)

    perm = _chunk_permutation(nb, S5_CHUNK_T)
    wb, av, wc = _s5_pack(s5_A_re[0], s5_A_im[0], s5_log_step[0], s5_B_re[0], s5_B_im[0],
                          s5_C_re[0], s5_C_im[0])
    gl = _s5_mixer(x, ctx, modtab[0], norm_mix[0][None, :], s5_w_in[0].astype(BF16),
                   jnp.asarray(perm, BF16), jnp.asarray(perm.T, BF16), wb, av, wc, s5_D[0][None, :])
    w1 = ffn_w_in.astype(BF16)
    w2 = ffn_w_out.astype(BF16)
    xc = _s5_tail(x, ctx, gl, modtab[0], norm_ffn[0][None, :],
                  s5_w_glu[0].astype(BF16), s5_w_out[0].astype(BF16), w1, w2)

    full_idx, nope_idx, pe_idx = _head_slot_maps()
    w_in = mla_w_in[0]
    w_in_slots = jnp.concatenate(
        [w_in[:, :MLA_QL + MLA_KVL], _take_padded(w_in[:, MLA_QL + MLA_KVL:], pe_idx)], axis=1)
    w_qbt = _take_padded(mla_w_q_b[0].reshape(MLA_QL, MLA_H, MLA_QK), full_idx)
    w_qbt = w_qbt.reshape(MLA_QL, MLA_H * HEAD_SLOT).T
    w_kvb = mla_w_kv_b[0].reshape(MLA_KVL, MLA_H, MLA_NOPE + MLA_V)
    w_kb = _take_padded(w_kvb[:, :, :MLA_NOPE], nope_idx).reshape(MLA_KVL, MLA_H * HEAD_SLOT)
    w_vt = w_kvb[:, :, MLA_NOPE:].reshape(MLA_KVL, MLA_H * MLA_V).T
    cos, sin = _rope_tables(n_ctx, seq)
    q_scale = (MLA_QK ** -0.5) * math.log2(math.e)
    g_q = _take_padded(mla_q_norm[0], full_idx)
    g_k = _take_padded(mla_k_norm[0], full_idx)
    gcq = ((g_q * q_scale) * cos[n_ctx:, :])[:, :MLA_QK].T
    gsq = ((jnp.roll(g_q, ROPE_PAIR) * q_scale) * sin[n_ctx:, :])[:, :MLA_QK].T
    gck = g_k * cos
    gsk = jnp.roll(g_k, ROPE_PAIR) * sin
    score_bound = 1.02 * MLA_QK * q_scale * jnp.max(jnp.abs(mla_q_norm[0])) * jnp.max(jnp.abs(mla_k_norm[0]))
    qpad = jnp.zeros((SUBLANES, ROW_BLOCK), F32).at[0, :].set(-score_bound)
    kpad = jnp.zeros((1, HEAD_SLOT), F32).at[0, PAD_LANE].set(1.0)
    qt, k, vt = _mla_project(
        xc, modtab[1], norm_mix[1][None, :], w_in_slots.astype(BF16),
        mla_q_a_norm[0][None, :], mla_kv_a_norm[0][None, :],
        w_qbt.astype(BF16), w_kb.astype(BF16), w_vt.astype(BF16),
        gcq, gsq, qpad, gck, gsk, kpad, n_ctx)
    att = lax.cond(score_bound <= SCORE_BOUND_LIMIT,
                   lambda: _attention(qt, k, vt, True),
                   lambda: _attention(qt, k, vt, False))
    return _mla_tail(xc, att, modtab[1], norm_ffn[1][None, :], mla_w_o[0].astype(BF16), w1, w2,
                     n_ctx)
```

```python
import functools
import math

import jax
import jax.numpy as jnp
import numpy as np
from jax import lax
from jax.experimental import pallas as pl
from jax.experimental.pallas import tpu as pltpu

F32 = jnp.float32
BF16 = jnp.bfloat16

D_MODEL = 1024
EPS = 1e-6
S5_GC = 16
S5_G = D_MODEL // S5_GC
S5_P = 64
S5_STATE = S5_G * S5_P
S5_GROUPS_PER_TILE = 16
S5_TILE_CH = S5_GROUPS_PER_TILE * S5_GC
S5_TILE_ST = S5_GROUPS_PER_TILE * S5_P
S5_NTILES = S5_G // S5_GROUPS_PER_TILE
S5_CHUNK_T = 32
MLA_H = 16
MLA_NOPE = 64
MLA_ROPE = 32
MLA_V = 64
MLA_QL = D_MODEL // 2
MLA_KVL = D_MODEL // 4
MLA_QK = MLA_NOPE + MLA_ROPE
HEAD_SLOT = 128
ROPE_PAIR = HEAD_SLOT // 2
ROPE_HALF = MLA_ROPE // 2
PAD_LANE = MLA_QK
SUBLANES = 8
GRID_W = 64
ROPE_BASE = 10000.0
FFN_HIDDEN = 2816
FFN_CHUNK = 256
ROW_BLOCK = 256
ATTN_Q_BLOCK = 256
ATTN_Q_STEP = 4 * ATTN_Q_BLOCK
ATTN_KEY_CHUNK = 1024
HEADS_PER_STEP = 2
VT_ROWS = MLA_V + 16
SCORE_BOUND_LIMIT = 50.0

VMEM_LIMIT = 56 * 1024 * 1024


def _const_spec(shape):
    nd = len(shape)
    return pl.BlockSpec(shape, lambda *_: (0,) * nd, pipeline_mode=pl.Buffered(1))


def _layer_spec(shape, layer):
    nd = len(shape) - 1
    return pl.BlockSpec((None,) + tuple(shape[1:]), lambda *_: (layer,) + (0,) * nd,
                        pipeline_mode=pl.Buffered(1))


def _rms(x, g):
    ms = jnp.mean(x * x, axis=-1, keepdims=True)
    return x * lax.rsqrt(ms + EPS) * g


def _modnorm(x, g, shift, scale):
    return _rms(x, g) * (1.0 + scale) + shift


def _bdot(a, b):
    return jnp.dot(a, b, preferred_element_type=F32)


def _bdot_nt(a, b):
    return lax.dot_general(a, b, (((1,), (1,)), ((), ())), preferred_element_type=F32)


def _silu(x):
    return x * jax.nn.sigmoid(x)


def _gelu_tanh(x):
    c = math.sqrt(2.0 / math.pi)
    return 0.5 * x * (1.0 + jnp.tanh(c * (x + 0.044715 * (x * x * x))))


def _mod_kernel(c_ref, w_ref, b_ref, o_ref):
    a = _silu(c_ref[...]).astype(BF16)
    o_ref[...] = _bdot(a, w_ref[...].astype(BF16)) + b_ref[...]


def _modulation(cvec, mod_w, mod_b):
    depth, d, n = mod_w.shape
    nb = 1536
    return pl.pallas_call(
        _mod_kernel,
        grid=(depth, n // nb),
        in_specs=[
            pl.BlockSpec((16, d), lambda i, j: (0, 0)),
            pl.BlockSpec((None, d, nb), lambda i, j: (i, 0, j)),
            pl.BlockSpec((None, 1, nb), lambda i, j: (i, 0, j)),
        ],
        out_specs=pl.BlockSpec((None, 16, nb), lambda i, j: (i, 0, j)),
        out_shape=jax.ShapeDtypeStruct((depth, 16, n), F32),
        compiler_params=pltpu.CompilerParams(vmem_limit_bytes=VMEM_LIMIT),
        name="modulation",
    )(cvec, mod_w, mod_b.reshape(depth, 1, n))


def _s5_scan_chunk(u, wb_ref, a_ref, wc_ref, bbuf, hstate, *, reverse):
    rows = u.shape[0]
    nb = hstate.shape[0]
    lc = rows // nb

    @pl.when(pl.program_id(0) == 0)
    def _():
        hstate[...] = jnp.zeros_like(hstate)

    ub = u.astype(BF16)

    def expand(j):
        bj = _bdot(ub[:, j * S5_TILE_CH:(j + 1) * S5_TILE_CH], wb_ref[j])
        bbuf[:, j * S5_TILE_ST:(j + 1) * S5_TILE_ST] = bj[:, :S5_TILE_ST]
        bbuf[:, S5_STATE + j * S5_TILE_ST:S5_STATE + (j + 1) * S5_TILE_ST] = bj[:, S5_TILE_ST:]

    ys = []
    expand(0)
    for j in range(S5_NTILES):
        if j + 1 < S5_NTILES:
            expand(j + 1)
        cr = slice(j * S5_TILE_ST, (j + 1) * S5_TILE_ST)
        ci = slice(S5_STATE + j * S5_TILE_ST, S5_STATE + (j + 1) * S5_TILE_ST)
        ar = jnp.broadcast_to(a_ref[0:1, cr], (nb, S5_TILE_ST))
        ai = jnp.broadcast_to(a_ref[1:2, cr], (nb, S5_TILE_ST))
        hr, hi = hstate[:, cr], hstate[:, ci]
        for i in range(lc):
            t = (lc - 1 - i) if reverse else i
            r = slice(t * nb, (t + 1) * nb)
            hr, hi = (ar * hr - ai * hi) + bbuf[r, cr], (ar * hi + ai * hr) + bbuf[r, ci]
            bbuf[r, cr] = hr
            bbuf[r, ci] = hi
        hstate[:, cr] = hr
        hstate[:, ci] = hi
        ys.append(_bdot(bbuf[:, cr].astype(BF16), wc_ref[j, 0:S5_TILE_ST, :])
                  + _bdot(bbuf[:, ci].astype(BF16), wc_ref[j, S5_TILE_ST:, :]))
    return jnp.concatenate(ys, axis=1)


def _s5_fwd_kernel(x_ref, ctx_ref, mod_ref, g_ref, win_ref, perm_ref, wb_ref, a_ref, wc_ref,
                   yf_ref, u_ref, bbuf, hstate, *, ctx_chunks):
    nb, lc, d = x_ref.shape
    xin = jnp.where(pl.program_id(0) < ctx_chunks, ctx_ref[...], x_ref[...])
    h = _modnorm(xin, g_ref[...], mod_ref[:, :, 0:d], mod_ref[:, :, d:2 * d])
    hb = h.reshape(nb * lc, d).astype(BF16)
    htb = _bdot(perm_ref[...], hb).astype(BF16)
    u = _bdot(htb, win_ref[...])
    u_ref[...] = u
    yf_ref[...] = _s5_scan_chunk(u, wb_ref, a_ref, wc_ref, bbuf, hstate, reverse=False)


def _s5_bwd_kernel(u_ref, yf_ref, wb_ref, a_ref, wc_ref, dskip_ref, permt_ref, o_ref, bbuf, hstate):
    u = u_ref[...]
    y = _s5_scan_chunk(u, wb_ref, a_ref, wc_ref, bbuf, hstate, reverse=True)
    y = y + yf_ref[...] + dskip_ref[...] * u
    gb = _gelu_tanh(y).astype(BF16)
    o_ref[...] = _bdot(permt_ref[...], gb).astype(BF16).reshape(o_ref.shape)


def _s5_mixer(x, ctx, modtab, norm_g, w_in, perm, permt, wb, a, wc, dskip):
    nb, seq, d = x.shape
    n_ctx = ctx.shape[1]
    lc = S5_CHUNK_T
    rows = nb * lc
    ctx_chunks = n_ctx // lc
    nchunks = ctx_chunks + seq // lc

    def bwd_chunk(i):
        return jnp.where(i < ctx_chunks, ctx_chunks - 1 - i, nchunks - 1 + ctx_chunks - i)

    def dir_specs(k):
        return [
            pl.BlockSpec((None, S5_NTILES, S5_TILE_CH, 2 * S5_TILE_ST), lambda i: (k, 0, 0, 0),
                         pipeline_mode=pl.Buffered(1)),
            pl.BlockSpec((None, 2, S5_STATE), lambda i: (k, 0, 0), pipeline_mode=pl.Buffered(1)),
            pl.BlockSpec((None, S5_NTILES, 2 * S5_TILE_ST, S5_TILE_CH), lambda i: (k, 0, 0, 0),
                         pipeline_mode=pl.Buffered(1)),
        ]

    scratch = [pltpu.VMEM((rows, 2 * S5_STATE), F32), pltpu.VMEM((nb, 2 * S5_STATE), F32)]
    params = pltpu.CompilerParams(dimension_semantics=("arbitrary",), vmem_limit_bytes=VMEM_LIMIT)

    chunk_rows = jax.ShapeDtypeStruct((nchunks, rows, d), F32)

    yf, u = pl.pallas_call(
        functools.partial(_s5_fwd_kernel, ctx_chunks=ctx_chunks),
        grid=(nchunks,),
        in_specs=[
            pl.BlockSpec((nb, lc, d), lambda i: (0, jnp.maximum(i - ctx_chunks, 0), 0)),
            pl.BlockSpec((nb, lc, d), lambda i: (0, jnp.minimum(i, ctx_chunks - 1), 0)),
            pl.BlockSpec((None, nb, 1, 6 * d), lambda i: ((i >= ctx_chunks).astype(jnp.int32), 0, 0, 0)),
            _const_spec((1, d)),
            _const_spec((d, d)),
            _const_spec((rows, rows)),
        ] + dir_specs(0),
        out_specs=[pl.BlockSpec((None, rows, d), lambda i: (i, 0, 0))] * 2,
        out_shape=[chunk_rows, chunk_rows],
        scratch_shapes=scratch,
        compiler_params=params,
        name="s5_forward",
    )(x, ctx, modtab, norm_g, w_in, perm, wb, a, wc)

    return pl.pallas_call(
        _s5_bwd_kernel,
        grid=(nchunks,),
        in_specs=[pl.BlockSpec((None, rows, d), lambda i: (bwd_chunk(i), 0, 0))] * 2 + dir_specs(1) + [
            _const_spec((1, d)),
            _const_spec((rows, rows)),
        ],
        out_specs=pl.BlockSpec((nb, lc, d), lambda i: (0, bwd_chunk(i), 0)),
        out_shape=jax.ShapeDtypeStruct((nb, n_ctx + seq, d), BF16),
        scratch_shapes=scratch,
        compiler_params=params,
        name="s5_backward",
    )(u, yf, wb, a, wc, dskip, permt)


def _ffn_residual(x1, mod_ref, g_ref, w1_ref, w2_ref):
    d = x1.shape[-1]
    hn = _modnorm(x1, g_ref[...], mod_ref[:, 3 * d:4 * d], mod_ref[:, 4 * d:5 * d]).astype(BF16)

    def up(k):
        lo = k * FFN_CHUNK
        return (_bdot(hn, w1_ref[:, lo:lo + FFN_CHUNK]),
                _bdot(hn, w1_ref[:, FFN_HIDDEN + lo:FFN_HIDDEN + lo + FFN_CHUNK]))

    nchunks = FFN_HIDDEN // FFN_CHUNK
    acc = jnp.zeros(x1.shape, F32)
    nxt = up(0)
    for k in range(nchunks):
        gk, uk = nxt
        if k + 1 < nchunks:
            nxt = up(k + 1)
        acc = acc + _bdot((_silu(gk) * uk).astype(BF16), w2_ref[k * FFN_CHUNK:(k + 1) * FFN_CHUNK, :])
    return x1 + mod_ref[:, 5 * d:6 * d] * acc


def _s5_tail_kernel(x_ref, ctx_ref, gl_ref, mod_ref, gf_ref, wglu_ref, wout_ref, w1_ref, w2_ref,
                    o_ref, *, ctx_blocks):
    d = x_ref.shape[-1]
    xin = jnp.where(pl.program_id(1) < ctx_blocks, ctx_ref[...], x_ref[...])
    z = _bdot(gl_ref[...], wglu_ref[...])
    gated = (z[:, :d] * jax.nn.sigmoid(z[:, d:])).astype(BF16)
    x1 = xin + mod_ref[:, 2 * d:3 * d] * _bdot(gated, wout_ref[...])
    o_ref[...] = _ffn_residual(x1, mod_ref, gf_ref, w1_ref, w2_ref)


def _mla_tail_kernel(x_ref, at_ref, mod_ref, gf_ref, wo_ref, w1_ref, w2_ref, o_ref):
    d = x_ref.shape[-1]
    x1 = x_ref[...] + mod_ref[:, 2 * d:3 * d] * _bdot(at_ref[...], wo_ref[...])
    o_ref[...] = _ffn_residual(x1, mod_ref, gf_ref, w1_ref, w2_ref)


_TAIL_PARAMS = pltpu.CompilerParams(
    dimension_semantics=("arbitrary", "arbitrary"), vmem_limit_bytes=VMEM_LIMIT)


def _s5_tail(x, ctx, gl, modtab, norm_g, w_glu, w_out, w1, w2):
    nb, seq, d = x.shape
    n_ctx = ctx.shape[1]
    rb = ROW_BLOCK
    ctx_blocks = n_ctx // rb
    nblk = ctx_blocks + seq // rb
    tok = pl.BlockSpec((None, rb, d), lambda b, i: (b, i, 0))
    return pl.pallas_call(
        functools.partial(_s5_tail_kernel, ctx_blocks=ctx_blocks),
        grid=(nb, nblk),
        in_specs=[
            pl.BlockSpec((None, rb, d), lambda b, i: (b, jnp.maximum(i - ctx_blocks, 0), 0)),
            pl.BlockSpec((None, rb, d), lambda b, i: (b, jnp.minimum(i, ctx_blocks - 1), 0)),
            tok,
            pl.BlockSpec((None, None, 1, 6 * d),
                         lambda b, i: ((i >= ctx_blocks).astype(jnp.int32), b, 0, 0)),
            _const_spec((1, d)),
            _const_spec(w_glu.shape), _const_spec(w_out.shape),
            _layer_spec(w1.shape, 0), _layer_spec(w2.shape, 0),
        ],
        out_specs=tok,
        out_shape=jax.ShapeDtypeStruct((nb, nblk * rb, d), F32),
        compiler_params=_TAIL_PARAMS,
        name="s5_tail",
    )(x, ctx, gl, modtab, norm_g, w_glu, w_out, w1, w2)


def _mla_tail(xc, att, modtab, norm_g, w_o, w1, w2, n_ctx):
    nb, s, d = xc.shape
    rb = ROW_BLOCK
    off = n_ctx // rb
    nblk = s // rb - off
    tok = pl.BlockSpec((None, rb, d), lambda b, i: (b, i, 0))
    return pl.pallas_call(
        _mla_tail_kernel,
        grid=(nb, nblk),
        in_specs=[
            pl.BlockSpec((None, rb, d), lambda b, i: (b, i + off, 0)),
            tok,
            pl.BlockSpec((None, None, 1, 6 * d), lambda b, i: (1, b, 0, 0)),
            _const_spec((1, d)),
            _const_spec(w_o.shape), _layer_spec(w1.shape, 1), _layer_spec(w2.shape, 1),
        ],
        out_specs=tok,
        out_shape=jax.ShapeDtypeStruct((nb, nblk * rb, d), F32),
        compiler_params=_TAIL_PARAMS,
        name="mla_tail",
    )(xc, att, modtab, norm_g, w_o, w1, w2)


def _mla_proj_kernel(x_ref, mod_ref, g_ref, win_ref, gqa_ref, gkva_ref, wqbt_ref, wkb_ref, wvt_ref,
                     gcq_ref, gsq_ref, qpad_ref, gck_ref, gsk_ref, kpad_ref,
                     qt_ref, k_ref, vt_ref):
    d = x_ref.shape[-1]
    h = _modnorm(x_ref[...], g_ref[...], mod_ref[:, 0:d], mod_ref[:, d:2 * d]).astype(BF16)
    proj = _bdot(h, win_ref[...])
    inv_qk = 1.0 / MLA_QK
    lo, hi = slice(0, ROPE_HALF), slice(ROPE_PAIR, ROPE_PAIR + ROPE_HALF)

    ql = _rms(proj[:, 0:MLA_QL], gqa_ref[...]).astype(BF16)
    ckv = _rms(proj[:, MLA_QL:MLA_QL + MLA_KVL], gkva_ref[...]).astype(BF16)
    qt = _bdot_nt(wqbt_ref[...], ql)
    knope = _bdot(ckv, wkb_ref[...])
    vt = _bdot_nt(wvt_ref[...], ckv)

    gc, gs = gcq_ref[...], gsq_ref[...]
    pad = jnp.zeros((HEAD_SLOT - MLA_QK - SUBLANES, qt.shape[1]), F32)
    for hd in range(MLA_H):
        seg = qt[hd * HEAD_SLOT:hd * HEAD_SLOT + MLA_QK, :]
        r = lax.rsqrt(jnp.sum(seg * seg, axis=0, keepdims=True) * inv_qk + EPS)
        rot = seg * gc
        rot = jnp.concatenate([rot[lo] + seg[hi] * gs[lo], rot[ROPE_HALF:ROPE_PAIR],
                               rot[hi] + seg[lo] * gs[hi], rot[ROPE_PAIR + ROPE_HALF:]], axis=0)
        out = jnp.concatenate([rot * r, qpad_ref[...], pad], axis=0)
        qt_ref[hd * HEAD_SLOT:(hd + 1) * HEAD_SLOT, :] = out.astype(BF16)

    kpe = proj[:, MLA_QL + MLA_KVL:]
    gck = gck_ref[...]
    rot_pe = pltpu.roll(kpe, ROPE_PAIR, 1) * gsk_ref[...]
    kpad = kpad_ref[...]
    for hd in range(MLA_H):
        t = knope[:, hd * HEAD_SLOT:(hd + 1) * HEAD_SLOT] + kpe
        r = lax.rsqrt(jnp.sum(t * t, axis=-1, keepdims=True) * inv_qk + EPS)
        k_ref[:, hd * HEAD_SLOT:(hd + 1) * HEAD_SLOT] = ((t * gck + rot_pe) * r + kpad).astype(BF16)

    ones = jnp.ones((VT_ROWS - MLA_V, vt.shape[1]), BF16)
    for hd in range(MLA_H):
        vt_ref[hd, 0:MLA_V, :] = vt[hd * MLA_V:(hd + 1) * MLA_V, :].astype(BF16)
        vt_ref[hd, MLA_V:, :] = ones


def _mla_project(xc, modtab, norm_g, w_in, g_qa, g_kva, w_qbt, w_kb, w_vt,
                 gcq, gsq, qpad, gck, gsk, kpad, n_ctx):
    nb, s, d = xc.shape
    rb = ROW_BLOCK
    ctx_blocks = n_ctx // rb
    qk_n = MLA_H * HEAD_SLOT
    qblk = lambda b, i: (b, 0, jnp.maximum(i - ctx_blocks, 0))
    return pl.pallas_call(
        _mla_proj_kernel,
        grid=(nb, s // rb),
        in_specs=[
            pl.BlockSpec((None, rb, d), lambda b, i: (b, i, 0)),
            pl.BlockSpec((None, None, 1, 6 * d),
                         lambda b, i: ((i >= ctx_blocks).astype(jnp.int32), b, 0, 0)),
            _const_spec((1, d)),
            _const_spec(w_in.shape),
            _const_spec(g_qa.shape),
            _const_spec(g_kva.shape),
            _const_spec(w_qbt.shape),
            _const_spec(w_kb.shape),
            _const_spec(w_vt.shape),
            pl.BlockSpec((MLA_QK, rb), lambda b, i: (0, jnp.maximum(i - ctx_blocks, 0))),
            pl.BlockSpec((MLA_QK, rb), lambda b, i: (0, jnp.maximum(i - ctx_blocks, 0))),
            _const_spec(qpad.shape),
            pl.BlockSpec((rb, HEAD_SLOT), lambda b, i: (i, 0)),
            pl.BlockSpec((rb, HEAD_SLOT), lambda b, i: (i, 0)),
            _const_spec(kpad.shape),
        ],
        out_specs=[
            pl.BlockSpec((None, qk_n, rb), qblk),
            pl.BlockSpec((None, rb, qk_n), lambda b, i: (b, i, 0)),
            pl.BlockSpec((None, MLA_H, VT_ROWS, rb), lambda b, i: (b, 0, 0, i)),
        ],
        out_shape=[
            jax.ShapeDtypeStruct((nb, qk_n, s - n_ctx), BF16),
            jax.ShapeDtypeStruct((nb, s, qk_n), BF16),
            jax.ShapeDtypeStruct((nb, MLA_H, VT_ROWS, s), BF16),
        ],
        compiler_params=pltpu.CompilerParams(
            dimension_semantics=("arbitrary", "arbitrary"), vmem_limit_bytes=VMEM_LIMIT),
        name="mla_project",
    )(xc, modtab, norm_g, w_in, g_qa, g_kva, w_qbt, w_kb, w_vt, gcq, gsq, qpad, gck, gsk, kpad)


def _attn_kernel(qt_ref, k_ref, vt_ref, o_ref, *, bounded):
    tq = ATTN_Q_BLOCK
    nkeys = k_ref.shape[0]
    bounds = list(range(0, nkeys, ATTN_KEY_CHUNK)) + [nkeys]
    nchunks = len(bounds) - 1
    zero = jnp.zeros((HEAD_SLOT, tq), BF16)

    for sub in range(ATTN_Q_STEP // tq):
        qs = slice(sub * tq, (sub + 1) * tq)
        qbd = jnp.concatenate(
            [jnp.concatenate([qt_ref[0:HEAD_SLOT, qs], zero], axis=1),
             jnp.concatenate([zero, qt_ref[HEAD_SLOT:, qs]], axis=1)], axis=0)

        def scores(c, qbd=qbd):
            return _bdot(k_ref[bounds[c]:bounds[c + 1], :], qbd)

        m = jnp.full((1, HEADS_PER_STEP * tq), -jnp.inf, F32)
        accs = [jnp.zeros((VT_ROWS, tq), F32) for _ in range(HEADS_PER_STEP)]
        s_next = scores(0)
        for c in range(nchunks):
            s = s_next
            if c + 1 < nchunks:
                s_next = scores(c + 1)
            if not bounded:
                m_new = jnp.maximum(m, jnp.max(s, axis=0, keepdims=True))
                alpha = jnp.exp2(m - m_new)
                s = s - m_new
                m = m_new
            pb = jnp.exp2(s).astype(BF16)
            for hd in range(HEADS_PER_STEP):
                pv = _bdot(vt_ref[hd, :, bounds[c]:bounds[c + 1]], pb[:, hd * tq:(hd + 1) * tq])
                if bounded:
                    accs[hd] = accs[hd] + pv
                else:
                    accs[hd] = accs[hd] * alpha[:, hd * tq:(hd + 1) * tq] + pv
        outs = [a[0:MLA_V, :] * (1.0 / a[MLA_V:MLA_V + 1, :]) for a in accs]
        o_ref[qs, :] = jnp.concatenate(outs, axis=0).T.astype(BF16)


def _attention(qt, k, vt, bounded):
    nb, _, seq = qt.shape
    s = k.shape[1]
    hp = MLA_H // HEADS_PER_STEP
    return pl.pallas_call(
        functools.partial(_attn_kernel, bounded=bounded),
        grid=(nb, hp, seq // ATTN_Q_STEP),
        in_specs=[
            pl.BlockSpec((None, HEADS_PER_STEP * HEAD_SLOT, ATTN_Q_STEP), lambda b, h, i: (b, h, i)),
            pl.BlockSpec((None, s, HEADS_PER_STEP * HEAD_SLOT), lambda b, h, i: (b, 0, h)),
            pl.BlockSpec((None, HEADS_PER_STEP, VT_ROWS, s), lambda b, h, i: (b, h, 0, 0)),
        ],
        out_specs=pl.BlockSpec((None, ATTN_Q_STEP, HEADS_PER_STEP * MLA_V), lambda b, h, i: (b, i, h)),
        out_shape=jax.ShapeDtypeStruct((nb, seq, MLA_H * MLA_V), BF16),
        compiler_params=pltpu.CompilerParams(
            dimension_semantics=("arbitrary", "arbitrary", "arbitrary"),
            vmem_limit_bytes=VMEM_LIMIT),
        name="mla_attention_bounded" if bounded else "mla_attention_online",
    )(qt, k, vt)


def _s5_discretize(A_re, A_im, log_step, B_re, B_im):
    lr = -jnp.abs(A_re)
    li = A_im
    dt = jnp.exp(log_step)[..., None]
    mag = jnp.exp(lr * dt)
    ar = mag * jnp.cos(li * dt)
    ai = mag * jnp.sin(li * dt)
    den = lr * lr + li * li
    cr = ((ar - 1) * lr + ai * li) / den
    ci = (ai * lr - (ar - 1) * li) / den
    bbr = cr[..., None] * B_re - ci[..., None] * B_im
    bbi = cr[..., None] * B_im + ci[..., None] * B_re
    return ar, ai, bbr, bbi


def _block_diag_tiles(w, inner):
    nt, rows, _ = w.shape
    gt = S5_GROUPS_PER_TILE
    tiled = jnp.tile(w, (1, 1, gt))
    rg = lax.broadcasted_iota(jnp.int32, (rows, gt * inner), 0) // (rows // gt)
    cg = lax.broadcasted_iota(jnp.int32, (rows, gt * inner), 1) // inner
    return jnp.where((rg == cg)[None], tiled, 0.0)


def _s5_pack(A_re, A_im, log_step, B_re, B_im, C_re, C_im):
    nt = S5_NTILES
    ar, ai, bbr, bbi = _s5_discretize(A_re, A_im, log_step, B_re, B_im)

    def tile_in(w):
        w = w.transpose(0, 1, 3, 2).reshape(2 * nt, S5_TILE_CH, S5_P)
        return _block_diag_tiles(w, S5_P)

    def tile_out(w):
        w = w.transpose(0, 1, 3, 2).reshape(2 * nt, S5_TILE_ST, S5_GC)
        return _block_diag_tiles(w, S5_GC)

    wb = jnp.concatenate([tile_in(bbr), tile_in(bbi)], axis=2).astype(BF16)
    wc = jnp.concatenate([tile_out(C_re), tile_out(-C_im)], axis=1).astype(BF16)
    av = jnp.stack([ar.reshape(2, -1), ai.reshape(2, -1)], axis=1)
    return (wb.reshape(2, nt, S5_TILE_CH, 2 * S5_TILE_ST), av,
            wc.reshape(2, nt, 2 * S5_TILE_ST, S5_TILE_CH))


def _chunk_permutation(nb, lc):
    r = np.arange(nb * lc)
    src = (r % nb) * lc + r // nb
    p = np.zeros((nb * lc, nb * lc), np.float32)
    p[r, src] = 1.0
    return p


def _head_slot_maps():
    full = np.full((HEAD_SLOT,), MLA_QK, np.int32)
    nope = np.full((HEAD_SLOT,), MLA_NOPE, np.int32)
    pe = np.full((HEAD_SLOT,), MLA_ROPE, np.int32)
    for dim in range(MLA_NOPE):
        lane = 16 + dim if dim < 48 else 32 + dim
        full[lane] = dim
        nope[lane] = dim
    half = MLA_ROPE // 4
    for e in range(MLA_ROPE):
        axis, hf, f = e // (2 * half), (e % (2 * half)) // half, e % half
        lane = (ROPE_PAIR if hf == 0 else 0) + axis * half + f
        full[lane] = MLA_NOPE + e
        pe[lane] = e
    return full, nope, pe


def _take_padded(w, idx):
    zero = jnp.zeros(w.shape[:-1] + (1,), w.dtype)
    return jnp.take(jnp.concatenate([w, zero], axis=-1), jnp.asarray(idx), axis=-1)


def _rope_tables(n_ctx, seq):
    half = MLA_ROPE // 4
    rows = seq // GRID_W
    row = jnp.repeat(jnp.arange(rows, dtype=F32), GRID_W)
    col = jnp.tile(jnp.arange(GRID_W, dtype=F32), rows)
    axis_dim = MLA_ROPE // 2
    inv_freq = ROPE_BASE ** (-jnp.arange(0, axis_dim, 2, dtype=F32) / axis_dim)
    ang = jnp.stack([row[:, None] * inv_freq, col[:, None] * inv_freq], axis=1).reshape(seq, 2 * half)
    c, s = jnp.cos(ang), jnp.sin(ang)
    one = jnp.ones((seq, ROPE_PAIR - 2 * half), F32)
    cos = jnp.concatenate([c, one, c, one], axis=1)
    sin = jnp.concatenate([s, 0.0 * one, -s, 0.0 * one], axis=1)
    cos = jnp.concatenate([jnp.ones((n_ctx, HEAD_SLOT), F32), cos], axis=0)
    sin = jnp.concatenate([jnp.zeros((n_ctx, HEAD_SLOT), F32), sin], axis=0)
    return cos, sin


def kernel(x, c, ctx, c_ctx, mod_w, mod_b, norm_mix, norm_ffn, ffn_w_in, ffn_w_out, s5_w_in, s5_A_re, s5_A_im, s5_log_step, s5_B_re, s5_B_im, s5_C_re, s5_C_im, s5_D, s5_w_glu, s5_w_out, mla_w_in, mla_q_a_norm, mla_kv_a_norm, mla_w_q_b, mla_w_kv_b, mla_q_norm, mla_k_norm, mla_w_o):
    nb, seq, d = x.shape
    n_ctx = ctx.shape[1]
    assert d == D_MODEL and nb == SUBLANES
    assert n_ctx % ROW_BLOCK == 0 and seq % ATTN_Q_STEP == 0 and seq % GRID_W == 0

    cvec = jnp.concatenate([c, c_ctx[None, :], jnp.zeros((16 - nb - 1, d), F32)], axis=0)
    mod = _modulation(cvec, mod_w, mod_b)
    mod_x = mod[:, :nb, None, :]
    mod_c = jnp.broadcast_to(mod[:, nb:nb + 1, None, :], mod_x.shape)
    modtab = jnp.stack([mod_c, mod_x], axis=1)

    perm = _chunk_permutation(nb, S5_CHUNK_T)
    wb, av, wc = _s5_pack(s5_A_re[0], s5_A_im[0], s5_log_step[0], s5_B_re[0], s5_B_im[0],
                          s5_C_re[0], s5_C_im[0])
    gl = _s5_mixer(x, ctx, modtab[0], norm_mix[0][None, :], s5_w_in[0].astype(BF16),
                   jnp.asarray(perm, BF16), jnp.asarray(perm.T, BF16), wb, av, wc, s5_D[0][None, :])
    w1 = ffn_w_in.astype(BF16)
    w2 = ffn_w_out.astype(BF16)
    xc = _s5_tail(x, ctx, gl, modtab[0], norm_ffn[0][None, :],
                  s5_w_glu[0].astype(BF16), s5_w_out[0].astype(BF16), w1, w2)

    full_idx, nope_idx, pe_idx = _head_slot_maps()
    w_in = mla_w_in[0]
    w_in_slots = jnp.concatenate(
        [w_in[:, :MLA_QL + MLA_KVL], _take_padded(w_in[:, MLA_QL + MLA_KVL:], pe_idx)], axis=1)
    w_qbt = _take_padded(mla_w_q_b[0].reshape(MLA_QL, MLA_H, MLA_QK), full_idx)
    w_qbt = w_qbt.reshape(MLA_QL, MLA_H * HEAD_SLOT).T
    w_kvb = mla_w_kv_b[0].reshape(MLA_KVL, MLA_H, MLA_NOPE + MLA_V)
    w_kb = _take_padded(w_kvb[:, :, :MLA_NOPE], nope_idx).reshape(MLA_KVL, MLA_H * HEAD_SLOT)
    w_vt = w_kvb[:, :, MLA_NOPE:].reshape(MLA_KVL, MLA_H * MLA_V).T
    cos, sin = _rope_tables(n_ctx, seq)
    q_scale = (MLA_QK ** -0.5) * math.log2(math.e)
    g_q = _take_padded(mla_q_norm[0], full_idx)
    g_k = _take_padded(mla_k_norm[0], full_idx)
    gcq = ((g_q * q_scale) * cos[n_ctx:, :])[:, :MLA_QK].T
    gsq = ((jnp.roll(g_q, ROPE_PAIR) * q_scale) * sin[n_ctx:, :])[:, :MLA_QK].T
    gck = g_k * cos
    gsk = jnp.roll(g_k, ROPE_PAIR) * sin
    score_bound = 1.02 * MLA_QK * q_scale * jnp.max(jnp.abs(mla_q_norm[0])) * jnp.max(jnp.abs(mla_k_norm[0]))
    qpad = jnp.zeros((SUBLANES, ROW_BLOCK), F32).at[0, :].set(-score_bound)
    kpad = jnp.zeros((1, HEAD_SLOT), F32).at[0, PAD_LANE].set(1.0)
    qt, k, vt = _mla_project(
        xc, modtab[1], norm_mix[1][None, :], w_in_slots.astype(BF16),
        mla_q_a_norm[0][None, :], mla_kv_a_norm[0][None, :],
        w_qbt.astype(BF16), w_kb.astype(BF16), w_vt.astype(BF16),
        gcq, gsq, qpad, gck, gsk, kpad, n_ctx)
    att = lax.cond(score_bound <= SCORE_BOUND_LIMIT,
                   lambda: _attention(qt, k, vt, True),
                   lambda: _attention(qt, k, vt, False))
    return _mla_tail(xc, att, modtab[1], norm_ffn[1][None, :], mla_w_o[0].astype(BF16), w1, w2,
                     n_ctx)
```

```python
import functools
import math

import jax
import jax.numpy as jnp
import numpy as np
from jax import lax
from jax.experimental import pallas as pl
from jax.experimental.pallas import tpu as pltpu

F32 = jnp.float32
BF16 = jnp.bfloat16

D_MODEL = 1024
EPS = 1e-6
S5_GC = 16
S5_G = D_MODEL // S5_GC
S5_P = 64
S5_STATE = S5_G * S5_P
S5_GROUPS_PER_TILE = 16
S5_TILE_CH = S5_GROUPS_PER_TILE * S5_GC
S5_TILE_ST = S5_GROUPS_PER_TILE * S5_P
S5_NTILES = S5_G // S5_GROUPS_PER_TILE
S5_CHUNK_T = 32
S5_CHUNKS_PER_STEP = 2
MLA_H = 16
MLA_NOPE = 64
MLA_ROPE = 32
MLA_V = 64
MLA_QL = D_MODEL // 2
MLA_KVL = D_MODEL // 4
MLA_QK = MLA_NOPE + MLA_ROPE
HEAD_SLOT = 128
ROPE_PAIR = HEAD_SLOT // 2
ROPE_HALF = MLA_ROPE // 2
SUBLANES = 8
GRID_W = 64
ROPE_BASE = 10000.0
FFN_HIDDEN = 2816
FFN_CHUNK = 256
ROW_BLOCK = 256
ATTN_Q_BLOCK = 256
ATTN_Q_STEP = 4 * ATTN_Q_BLOCK
ATTN_KEY_CHUNK = 1024
HEADS_PER_STEP = 2
VT_ROWS = MLA_V + 16
SCORE_BOUND_LIMIT = 50.0

VMEM_LIMIT = 56 * 1024 * 1024


def _const_spec(shape):
    nd = len(shape)
    return pl.BlockSpec(shape, lambda *_: (0,) * nd, pipeline_mode=pl.Buffered(1))


def _layer_spec(shape, layer):
    nd = len(shape) - 1
    return pl.BlockSpec((None,) + tuple(shape[1:]), lambda *_: (layer,) + (0,) * nd,
                        pipeline_mode=pl.Buffered(1))


def _rms(x, g):
    ms = jnp.mean(x * x, axis=-1, keepdims=True)
    return x * lax.rsqrt(ms + EPS) * g


def _modnorm(x, g, shift, scale):
    return _rms(x, g) * (1.0 + scale) + shift


def _bdot(a, b):
    return jnp.dot(a, b, preferred_element_type=F32)


def _bdot_nt(a, b):
    return lax.dot_general(a, b, (((1,), (1,)), ((), ())), preferred_element_type=F32)


def _silu(x):
    return x * jax.nn.sigmoid(x)


def _gelu_tanh(x):
    c = math.sqrt(2.0 / math.pi)
    return 0.5 * x * (1.0 + jnp.tanh(c * (x + 0.044715 * (x * x * x))))


def _mod_kernel(c_ref, w_ref, b_ref, o_ref):
    a = _silu(c_ref[...]).astype(BF16)
    o_ref[...] = _bdot(a, w_ref[...].astype(BF16)) + b_ref[...]


def _modulation(cvec, mod_w, mod_b):
    depth, d, n = mod_w.shape
    nb = 1536
    return pl.pallas_call(
        _mod_kernel,
        grid=(depth, n // nb),
        in_specs=[
            pl.BlockSpec((16, d), lambda i, j: (0, 0)),
            pl.BlockSpec((None, d, nb), lambda i, j: (i, 0, j)),
            pl.BlockSpec((None, 1, nb), lambda i, j: (i, 0, j)),
        ],
        out_specs=pl.BlockSpec((None, 16, nb), lambda i, j: (i, 0, j)),
        out_shape=jax.ShapeDtypeStruct((depth, 16, n), F32),
        compiler_params=pltpu.CompilerParams(vmem_limit_bytes=VMEM_LIMIT),
        name="modulation",
    )(cvec, mod_w, mod_b.reshape(depth, 1, n))


def _s5_scan_chunk(u, wb_ref, a_ref, wc_ref, bbuf, hstate, *, reverse):
    rows = u.shape[0]
    nb = hstate.shape[0]
    lc = rows // nb
    ub = u.astype(BF16)

    def expand(j):
        bj = _bdot(ub[:, j * S5_TILE_CH:(j + 1) * S5_TILE_CH], wb_ref[j])
        bbuf[:, j * S5_TILE_ST:(j + 1) * S5_TILE_ST] = bj[:, :S5_TILE_ST]
        bbuf[:, S5_STATE + j * S5_TILE_ST:S5_STATE + (j + 1) * S5_TILE_ST] = bj[:, S5_TILE_ST:]

    ys = []
    expand(0)
    for j in range(S5_NTILES):
        if j + 1 < S5_NTILES:
            expand(j + 1)
        cr = slice(j * S5_TILE_ST, (j + 1) * S5_TILE_ST)
        ci = slice(S5_STATE + j * S5_TILE_ST, S5_STATE + (j + 1) * S5_TILE_ST)
        ar = jnp.broadcast_to(a_ref[0:1, cr], (nb, S5_TILE_ST))
        ai = jnp.broadcast_to(a_ref[1:2, cr], (nb, S5_TILE_ST))
        hr, hi = hstate[:, cr], hstate[:, ci]
        for i in range(lc):
            t = (lc - 1 - i) if reverse else i
            r = slice(t * nb, (t + 1) * nb)
            hr, hi = (ar * hr - ai * hi) + bbuf[r, cr], (ar * hi + ai * hr) + bbuf[r, ci]
            bbuf[r, cr] = hr
            bbuf[r, ci] = hi
        hstate[:, cr] = hr
        hstate[:, ci] = hi
        ys.append(_bdot(bbuf[:, cr].astype(BF16), wc_ref[j, 0:S5_TILE_ST, :])
                  + _bdot(bbuf[:, ci].astype(BF16), wc_ref[j, S5_TILE_ST:, :]))
    return jnp.concatenate(ys, axis=1)


def _s5_zero_state_at_start(hstate):
    @pl.when(pl.program_id(0) == 0)
    def _():
        hstate[...] = jnp.zeros_like(hstate)


def _s5_fwd_kernel(x_ref, ctx_ref, mod_ref, g_ref, win_ref, perm_ref, wb_ref, a_ref, wc_ref,
                   yf_ref, u_ref, *scratch, ctx_steps):
    nb, _, d = x_ref.shape
    lc = S5_CHUNK_T
    bbufs, hstate = scratch[:-1], scratch[-1]
    _s5_zero_state_at_start(hstate)
    is_ctx = pl.program_id(0) < ctx_steps
    us = []
    for sub in range(S5_CHUNKS_PER_STEP):
        ts = slice(sub * lc, (sub + 1) * lc)
        xin = jnp.where(is_ctx, ctx_ref[:, ts, :], x_ref[:, ts, :])
        h = _modnorm(xin, g_ref[...], mod_ref[:, :, 0:d], mod_ref[:, :, d:2 * d])
        hb = h.reshape(nb * lc, d).astype(BF16)
        htb = _bdot(perm_ref[...], hb).astype(BF16)
        us.append(_bdot(htb, win_ref[...]))
        u_ref[sub] = us[sub]
    for sub, bbuf in enumerate(bbufs):
        yf_ref[sub] = _s5_scan_chunk(us[sub], wb_ref, a_ref, wc_ref, bbuf, hstate, reverse=False)


def _s5_bwd_kernel(u_ref, yf_ref, wb_ref, a_ref, wc_ref, dskip_ref, permt_ref, o_ref, *scratch):
    lc = S5_CHUNK_T
    bbufs, hstate = scratch[:-1], scratch[-1]
    _s5_zero_state_at_start(hstate)
    for sub, bbuf in reversed(list(enumerate(bbufs))):
        u = u_ref[sub]
        y = _s5_scan_chunk(u, wb_ref, a_ref, wc_ref, bbuf, hstate, reverse=True)
        y = y + yf_ref[sub] + dskip_ref[...] * u
        gb = _gelu_tanh(y).astype(BF16)
        o_ref[:, sub * lc:(sub + 1) * lc, :] = _bdot(permt_ref[...], gb).astype(BF16).reshape(
            o_ref.shape[0], lc, o_ref.shape[2])


def _s5_mixer(x, ctx, modtab, norm_g, w_in, perm, permt, wb, a, wc, dskip):
    nb, seq, d = x.shape
    n_ctx = ctx.shape[1]
    lc = S5_CHUNK_T
    rows = nb * lc
    cps = S5_CHUNKS_PER_STEP
    ls = cps * lc
    ctx_steps = n_ctx // ls
    nsteps = ctx_steps + seq // ls

    def bwd_step(i):
        return jnp.where(i < ctx_steps, ctx_steps - 1 - i, nsteps - 1 + ctx_steps - i)

    def dir_specs(k):
        return [
            pl.BlockSpec((None, S5_NTILES, S5_TILE_CH, 2 * S5_TILE_ST), lambda i: (k, 0, 0, 0),
                         pipeline_mode=pl.Buffered(1)),
            pl.BlockSpec((None, 2, S5_STATE), lambda i: (k, 0, 0), pipeline_mode=pl.Buffered(1)),
            pl.BlockSpec((None, S5_NTILES, 2 * S5_TILE_ST, S5_TILE_CH), lambda i: (k, 0, 0, 0),
                         pipeline_mode=pl.Buffered(1)),
        ]

    scratch = ([pltpu.VMEM((rows, 2 * S5_STATE), F32)] * cps
               + [pltpu.VMEM((nb, 2 * S5_STATE), F32)])
    params = pltpu.CompilerParams(dimension_semantics=("arbitrary",), vmem_limit_bytes=VMEM_LIMIT)

    chunk_rows = jax.ShapeDtypeStruct((nsteps * cps, rows, d), F32)

    yf, u = pl.pallas_call(
        functools.partial(_s5_fwd_kernel, ctx_steps=ctx_steps),
        grid=(nsteps,),
        in_specs=[
            pl.BlockSpec((nb, ls, d), lambda i: (0, jnp.maximum(i - ctx_steps, 0), 0)),
            pl.BlockSpec((nb, ls, d), lambda i: (0, jnp.minimum(i, ctx_steps - 1), 0)),
            pl.BlockSpec((None, nb, 1, 6 * d), lambda i: ((i >= ctx_steps).astype(jnp.int32), 0, 0, 0)),
            _const_spec((1, d)),
            _const_spec((d, d)),
            _const_spec((rows, rows)),
        ] + dir_specs(0),
        out_specs=[pl.BlockSpec((cps, rows, d), lambda i: (i, 0, 0))] * 2,
        out_shape=[chunk_rows, chunk_rows],
        scratch_shapes=scratch,
        compiler_params=params,
        name="s5_forward",
    )(x, ctx, modtab, norm_g, w_in, perm, wb, a, wc)

    return pl.pallas_call(
        _s5_bwd_kernel,
        grid=(nsteps,),
        in_specs=[pl.BlockSpec((cps, rows, d), lambda i: (bwd_step(i), 0, 0))] * 2 + dir_specs(1) + [
            _const_spec((1, d)),
            _const_spec((rows, rows)),
        ],
        out_specs=pl.BlockSpec((nb, ls, d), lambda i: (0, bwd_step(i), 0)),
        out_shape=jax.ShapeDtypeStruct((nb, n_ctx + seq, d), BF16),
        scratch_shapes=scratch,
        compiler_params=params,
        name="s5_backward",
    )(u, yf, wb, a, wc, dskip, permt)


def _ffn_residual(x1, mod_ref, g_ref, w1_ref, w2_ref):
    d = x1.shape[-1]
    hn = _modnorm(x1, g_ref[...], mod_ref[:, 3 * d:4 * d], mod_ref[:, 4 * d:5 * d]).astype(BF16)

    def up(k):
        lo = k * FFN_CHUNK
        return (_bdot(hn, w1_ref[:, lo:lo + FFN_CHUNK]),
                _bdot(hn, w1_ref[:, FFN_HIDDEN + lo:FFN_HIDDEN + lo + FFN_CHUNK]))

    nchunks = FFN_HIDDEN // FFN_CHUNK
    acc = jnp.zeros(x1.shape, F32)
    nxt = up(0)
    for k in range(nchunks):
        gk, uk = nxt
        if k + 1 < nchunks:
            nxt = up(k + 1)
        acc = acc + _bdot((_silu(gk) * uk).astype(BF16), w2_ref[k * FFN_CHUNK:(k + 1) * FFN_CHUNK, :])
    return x1 + mod_ref[:, 5 * d:6 * d] * acc


def _s5_tail_kernel(x_ref, ctx_ref, gl_ref, mod_ref, gf_ref, wglu_ref, wout_ref, w1_ref, w2_ref,
                    o_ref, *, ctx_blocks):
    d = x_ref.shape[-1]
    xin = jnp.where(pl.program_id(1) < ctx_blocks, ctx_ref[...], x_ref[...])
    z = _bdot(gl_ref[...], wglu_ref[...])
    gated = (z[:, :d] * jax.nn.sigmoid(z[:, d:])).astype(BF16)
    x1 = xin + mod_ref[:, 2 * d:3 * d] * _bdot(gated, wout_ref[...])
    o_ref[...] = _ffn_residual(x1, mod_ref, gf_ref, w1_ref, w2_ref)


def _mla_tail_kernel(x_ref, at_ref, mod_ref, gf_ref, wo_ref, w1_ref, w2_ref, o_ref):
    d = x_ref.shape[-1]
    x1 = x_ref[...] + mod_ref[:, 2 * d:3 * d] * _bdot(at_ref[...], wo_ref[...])
    o_ref[...] = _ffn_residual(x1, mod_ref, gf_ref, w1_ref, w2_ref)


_TAIL_PARAMS = pltpu.CompilerParams(
    dimension_semantics=("arbitrary", "arbitrary"), vmem_limit_bytes=VMEM_LIMIT)


def _s5_tail(x, ctx, gl, modtab, norm_g, w_glu, w_out, w1, w2):
    nb, seq, d = x.shape
    n_ctx = ctx.shape[1]
    rb = ROW_BLOCK
    ctx_blocks = n_ctx // rb
    nblk = ctx_blocks + seq // rb
    tok = pl.BlockSpec((None, rb, d), lambda b, i: (b, i, 0))
    return pl.pallas_call(
        functools.partial(_s5_tail_kernel, ctx_blocks=ctx_blocks),
        grid=(nb, nblk),
        in_specs=[
            pl.BlockSpec((None, rb, d), lambda b, i: (b, jnp.maximum(i - ctx_blocks, 0), 0)),
            pl.BlockSpec((None, rb, d), lambda b, i: (b, jnp.minimum(i, ctx_blocks - 1), 0)),
            tok,
            pl.BlockSpec((None, None, 1, 6 * d),
                         lambda b, i: ((i >= ctx_blocks).astype(jnp.int32), b, 0, 0)),
            _const_spec((1, d)),
            _const_spec(w_glu.shape), _const_spec(w_out.shape),
            _layer_spec(w1.shape, 0), _layer_spec(w2.shape, 0),
        ],
        out_specs=tok,
        out_shape=jax.ShapeDtypeStruct((nb, nblk * rb, d), F32),
        compiler_params=_TAIL_PARAMS,
        name="s5_tail",
    )(x, ctx, gl, modtab, norm_g, w_glu, w_out, w1, w2)


def _mla_tail(xc, att, modtab, norm_g, w_o, w1, w2, n_ctx):
    nb, s, d = xc.shape
    rb = ROW_BLOCK
    off = n_ctx // rb
    nblk = s // rb - off
    tok = pl.BlockSpec((None, rb, d), lambda b, i: (b, i, 0))
    return pl.pallas_call(
        _mla_tail_kernel,
        grid=(nb, nblk),
        in_specs=[
            pl.BlockSpec((None, rb, d), lambda b, i: (b, i + off, 0)),
            tok,
            pl.BlockSpec((None, None, 1, 6 * d), lambda b, i: (1, b, 0, 0)),
            _const_spec((1, d)),
            _const_spec(w_o.shape), _layer_spec(w1.shape, 1), _layer_spec(w2.shape, 1),
        ],
        out_specs=tok,
        out_shape=jax.ShapeDtypeStruct((nb, nblk * rb, d), F32),
        compiler_params=_TAIL_PARAMS,
        name="mla_tail",
    )(xc, att, modtab, norm_g, w_o, w1, w2)


def _mla_proj_kernel(x_ref, mod_ref, g_ref, win_ref, gqa_ref, gkva_ref, wqbt_ref, wkb_ref, wvt_ref,
                     gcq_ref, gsq_ref, gck_ref, gsk_ref,
                     qt_ref, k_ref, vt_ref):
    d = x_ref.shape[-1]
    h = _modnorm(x_ref[...], g_ref[...], mod_ref[:, 0:d], mod_ref[:, d:2 * d]).astype(BF16)
    proj = _bdot(h, win_ref[...])
    inv_qk = 1.0 / MLA_QK
    lo, hi = slice(0, ROPE_HALF), slice(ROPE_PAIR, ROPE_PAIR + ROPE_HALF)

    ql = _rms(proj[:, 0:MLA_QL], gqa_ref[...]).astype(BF16)
    ckv = _rms(proj[:, MLA_QL:MLA_QL + MLA_KVL], gkva_ref[...]).astype(BF16)
    qt = _bdot_nt(wqbt_ref[...], ql)
    knope = _bdot(ckv, wkb_ref[...])
    vt = _bdot_nt(wvt_ref[...], ckv)

    gc, gs = gcq_ref[...], gsq_ref[...]
    pad = jnp.zeros((HEAD_SLOT - MLA_QK, qt.shape[1]), F32)
    for hd in range(MLA_H):
        seg = qt[hd * HEAD_SLOT:hd * HEAD_SLOT + MLA_QK, :]
        r = lax.rsqrt(jnp.sum(seg * seg, axis=0, keepdims=True) * inv_qk + EPS)
        rot = seg * gc
        rot = jnp.concatenate([rot[lo] + seg[hi] * gs[lo], rot[ROPE_HALF:ROPE_PAIR],
                               rot[hi] + seg[lo] * gs[hi], rot[ROPE_PAIR + ROPE_HALF:]], axis=0)
        out = jnp.concatenate([rot * r, pad], axis=0)
        qt_ref[hd * HEAD_SLOT:(hd + 1) * HEAD_SLOT, :] = out.astype(BF16)

    kpe = proj[:, MLA_QL + MLA_KVL:]
    gck = gck_ref[...]
    rot_pe = pltpu.roll(kpe, ROPE_PAIR, 1) * gsk_ref[...]
    for hd in range(MLA_H):
        t = knope[:, hd * HEAD_SLOT:(hd + 1) * HEAD_SLOT] + kpe
        r = lax.rsqrt(jnp.sum(t * t, axis=-1, keepdims=True) * inv_qk + EPS)
        k_ref[:, hd * HEAD_SLOT:(hd + 1) * HEAD_SLOT] = ((t * gck + rot_pe) * r).astype(BF16)

    ones = jnp.ones((VT_ROWS - MLA_V, vt.shape[1]), BF16)
    for hd in range(MLA_H):
        vt_ref[hd, 0:MLA_V, :] = vt[hd * MLA_V:(hd + 1) * MLA_V, :].astype(BF16)
        vt_ref[hd, MLA_V:, :] = ones


def _mla_project(xc, modtab, norm_g, w_in, g_qa, g_kva, w_qbt, w_kb, w_vt,
                 gcq, gsq, gck, gsk, n_ctx):
    nb, s, d = xc.shape
    rb = ROW_BLOCK
    ctx_blocks = n_ctx // rb
    qk_n = MLA_H * HEAD_SLOT
    qblk = lambda b, i: (b, 0, jnp.maximum(i - ctx_blocks, 0))
    return pl.pallas_call(
        _mla_proj_kernel,
        grid=(nb, s // rb),
        in_specs=[
            pl.BlockSpec((None, rb, d), lambda b, i: (b, i, 0)),
            pl.BlockSpec((None, None, 1, 6 * d),
                         lambda b, i: ((i >= ctx_blocks).astype(jnp.int32), b, 0, 0)),
            _const_spec((1, d)),
            _const_spec(w_in.shape),
            _const_spec(g_qa.shape),
            _const_spec(g_kva.shape),
            _const_spec(w_qbt.shape),
            _const_spec(w_kb.shape),
            _const_spec(w_vt.shape),
            pl.BlockSpec((MLA_QK, rb), lambda b, i: (0, jnp.maximum(i - ctx_blocks, 0))),
            pl.BlockSpec((MLA_QK, rb), lambda b, i: (0, jnp.maximum(i - ctx_blocks, 0))),
            pl.BlockSpec((rb, HEAD_SLOT), lambda b, i: (i, 0)),
            pl.BlockSpec((rb, HEAD_SLOT), lambda b, i: (i, 0)),
        ],
        out_specs=[
            pl.BlockSpec((None, qk_n, rb), qblk),
            pl.BlockSpec((None, rb, qk_n), lambda b, i: (b, i, 0)),
            pl.BlockSpec((None, MLA_H, VT_ROWS, rb), lambda b, i: (b, 0, 0, i)),
        ],
        out_shape=[
            jax.ShapeDtypeStruct((nb, qk_n, s - n_ctx), BF16),
            jax.ShapeDtypeStruct((nb, s, qk_n), BF16),
            jax.ShapeDtypeStruct((nb, MLA_H, VT_ROWS, s), BF16),
        ],
        compiler_params=pltpu.CompilerParams(
            dimension_semantics=("arbitrary", "arbitrary"), vmem_limit_bytes=VMEM_LIMIT),
        name="mla_project",
    )(xc, modtab, norm_g, w_in, g_qa, g_kva, w_qbt, w_kb, w_vt, gcq, gsq, gck, gsk)


def _attn_kernel(qt_ref, k_ref, vt_ref, o_ref, *, bounded):
    tq = ATTN_Q_BLOCK
    nkeys = k_ref.shape[0]
    bounds = list(range(0, nkeys, ATTN_KEY_CHUNK)) + [nkeys]
    nchunks = len(bounds) - 1
    zero = jnp.zeros((HEAD_SLOT, tq), BF16)

    for sub in range(ATTN_Q_STEP // tq):
        qs = slice(sub * tq, (sub + 1) * tq)
        qbd = jnp.concatenate(
            [jnp.concatenate([qt_ref[0:HEAD_SLOT, qs], zero], axis=1),
             jnp.concatenate([zero, qt_ref[HEAD_SLOT:, qs]], axis=1)], axis=0)

        def scores(c, qbd=qbd):
            return _bdot(k_ref[bounds[c]:bounds[c + 1], :], qbd)

        m = jnp.full((1, HEADS_PER_STEP * tq), -jnp.inf, F32)
        accs = [jnp.zeros((VT_ROWS, tq), F32) for _ in range(HEADS_PER_STEP)]
        s_next = scores(0)
        for c in range(nchunks):
            s = s_next
            if c + 1 < nchunks:
                s_next = scores(c + 1)
            if not bounded:
                m_new = jnp.maximum(m, jnp.max(s, axis=0, keepdims=True))
                alpha = jnp.exp2(m - m_new)
                s = s - m_new
                m = m_new
            pb = jnp.exp2(s).astype(BF16)
            for hd in range(HEADS_PER_STEP):
                pv = _bdot(vt_ref[hd, :, bounds[c]:bounds[c + 1]], pb[:, hd * tq:(hd + 1) * tq])
                if bounded:
                    accs[hd] = accs[hd] + pv
                else:
                    accs[hd] = accs[hd] * alpha[:, hd * tq:(hd + 1) * tq] + pv
        outs = [a[0:MLA_V, :] * (1.0 / a[MLA_V:MLA_V + 1, :]) for a in accs]
        o_ref[qs, :] = jnp.concatenate(outs, axis=0).T.astype(BF16)


def _attention(qt, k, vt, bounded):
    nb, _, seq = qt.shape
    s = k.shape[1]
    hp = MLA_H // HEADS_PER_STEP
    return pl.pallas_call(
        functools.partial(_attn_kernel, bounded=bounded),
        grid=(nb, hp, seq // ATTN_Q_STEP),
        in_specs=[
            pl.BlockSpec((None, HEADS_PER_STEP * HEAD_SLOT, ATTN_Q_STEP), lambda b, h, i: (b, h, i)),
            pl.BlockSpec((None, s, HEADS_PER_STEP * HEAD_SLOT), lambda b, h, i: (b, 0, h)),
            pl.BlockSpec((None, HEADS_PER_STEP, VT_ROWS, s), lambda b, h, i: (b, h, 0, 0)),
        ],
        out_specs=pl.BlockSpec((None, ATTN_Q_STEP, HEADS_PER_STEP * MLA_V), lambda b, h, i: (b, i, h)),
        out_shape=jax.ShapeDtypeStruct((nb, seq, MLA_H * MLA_V), BF16),
        compiler_params=pltpu.CompilerParams(
            dimension_semantics=("arbitrary", "arbitrary", "arbitrary"),
            vmem_limit_bytes=VMEM_LIMIT),
        name="mla_attention_bounded" if bounded else "mla_attention_online",
    )(qt, k, vt)


def _s5_discretize(A_re, A_im, log_step, B_re, B_im):
    lr = -jnp.abs(A_re)
    li = A_im
    dt = jnp.exp(log_step)[..., None]
    mag = jnp.exp(lr * dt)
    ar = mag * jnp.cos(li * dt)
    ai = mag * jnp.sin(li * dt)
    den = lr * lr + li * li
    cr = ((ar - 1) * lr + ai * li) / den
    ci = (ai * lr - (ar - 1) * li) / den
    bbr = cr[..., None] * B_re - ci[..., None] * B_im
    bbi = cr[..., None] * B_im + ci[..., None] * B_re
    return ar, ai, bbr, bbi


def _block_diag_tiles(w, inner):
    nt, rows, _ = w.shape
    gt = S5_GROUPS_PER_TILE
    tiled = jnp.tile(w, (1, 1, gt))
    rg = lax.broadcasted_iota(jnp.int32, (rows, gt * inner), 0) // (rows // gt)
    cg = lax.broadcasted_iota(jnp.int32, (rows, gt * inner), 1) // inner
    return jnp.where((rg == cg)[None], tiled, 0.0)


def _s5_pack(A_re, A_im, log_step, B_re, B_im, C_re, C_im):
    nt = S5_NTILES
    ar, ai, bbr, bbi = _s5_discretize(A_re, A_im, log_step, B_re, B_im)

    def tile_in(w):
        w = w.transpose(0, 1, 3, 2).reshape(2 * nt, S5_TILE_CH, S5_P)
        return _block_diag_tiles(w, S5_P)

    def tile_out(w):
        w = w.transpose(0, 1, 3, 2).reshape(2 * nt, S5_TILE_ST, S5_GC)
        return _block_diag_tiles(w, S5_GC)

    wb = jnp.concatenate([tile_in(bbr), tile_in(bbi)], axis=2).astype(BF16)
    wc = jnp.concatenate([tile_out(C_re), tile_out(-C_im)], axis=1).astype(BF16)
    av = jnp.stack([ar.reshape(2, -1), ai.reshape(2, -1)], axis=1)
    return (wb.reshape(2, nt, S5_TILE_CH, 2 * S5_TILE_ST), av,
            wc.reshape(2, nt, 2 * S5_TILE_ST, S5_TILE_CH))


def _chunk_permutation(nb, lc):
    r = np.arange(nb * lc)
    src = (r % nb) * lc + r // nb
    p = np.zeros((nb * lc, nb * lc), np.float32)
    p[r, src] = 1.0
    return p


def _head_slot_maps():
    full = np.full((HEAD_SLOT,), MLA_QK, np.int32)
    nope = np.full((HEAD_SLOT,), MLA_NOPE, np.int32)
    pe = np.full((HEAD_SLOT,), MLA_ROPE, np.int32)
    for dim in range(MLA_NOPE):
        lane = 16 + dim if dim < 48 else 32 + dim
        full[lane] = dim
        nope[lane] = dim
    half = MLA_ROPE // 4
    for e in range(MLA_ROPE):
        axis, hf, f = e // (2 * half), (e % (2 * half)) // half, e % half
        lane = (ROPE_PAIR if hf == 0 else 0) + axis * half + f
        full[lane] = MLA_NOPE + e
        pe[lane] = e
    return full, nope, pe


def _take_padded(w, idx):
    zero = jnp.zeros(w.shape[:-1] + (1,), w.dtype)
    return jnp.take(jnp.concatenate([w, zero], axis=-1), jnp.asarray(idx), axis=-1)


def _rope_tables(n_ctx, seq):
    half = MLA_ROPE // 4
    rows = seq // GRID_W
    row = jnp.repeat(jnp.arange(rows, dtype=F32), GRID_W)
    col = jnp.tile(jnp.arange(GRID_W, dtype=F32), rows)
    axis_dim = MLA_ROPE // 2
    inv_freq = ROPE_BASE ** (-jnp.arange(0, axis_dim, 2, dtype=F32) / axis_dim)
    ang = jnp.stack([row[:, None] * inv_freq, col[:, None] * inv_freq], axis=1).reshape(seq, 2 * half)
    c, s = jnp.cos(ang), jnp.sin(ang)
    one = jnp.ones((seq, ROPE_PAIR - 2 * half), F32)
    cos = jnp.concatenate([c, one, c, one], axis=1)
    sin = jnp.concatenate([s, 0.0 * one, -s, 0.0 * one], axis=1)
    cos = jnp.concatenate([jnp.ones((n_ctx, HEAD_SLOT), F32), cos], axis=0)
    sin = jnp.concatenate([jnp.zeros((n_ctx, HEAD_SLOT), F32), sin], axis=0)
    return cos, sin


def kernel(x, c, ctx, c_ctx, mod_w, mod_b, norm_mix, norm_ffn, ffn_w_in, ffn_w_out, s5_w_in, s5_A_re, s5_A_im, s5_log_step, s5_B_re, s5_B_im, s5_C_re, s5_C_im, s5_D, s5_w_glu, s5_w_out, mla_w_in, mla_q_a_norm, mla_kv_a_norm, mla_w_q_b, mla_w_kv_b, mla_q_norm, mla_k_norm, mla_w_o):
    nb, seq, d = x.shape
    n_ctx = ctx.shape[1]
    assert d == D_MODEL and nb == SUBLANES
    assert n_ctx % ROW_BLOCK == 0 and seq % ATTN_Q_STEP == 0 and seq % GRID_W == 0

    cvec = jnp.concatenate([c, c_ctx[None, :], jnp.zeros((16 - nb - 1, d), F32)], axis=0)
    mod = _modulation(cvec, mod_w, mod_b)
    mod_x = mod[:, :nb, None, :]
    mod_c = jnp.broadcast_to(mod[:, nb:nb + 1, None, :], mod_x.shape)
    modtab = jnp.stack([mod_c, mod_x], axis=1)

    perm = _chunk_permutation(nb, S5_CHUNK_T)
    wb, av, wc = _s5_pack(s5_A_re[0], s5_A_im[0], s5_log_step[0], s5_B_re[0], s5_B_im[0],
                          s5_C_re[0], s5_C_im[0])
    gl = _s5_mixer(x, ctx, modtab[0], norm_mix[0][None, :], s5_w_in[0].astype(BF16),
                   jnp.asarray(perm, BF16), jnp.asarray(perm.T, BF16), wb, av, wc, s5_D[0][None, :])
    w1 = ffn_w_in.astype(BF16)
    w2 = ffn_w_out.astype(BF16)
    xc = _s5_tail(x, ctx, gl, modtab[0], norm_ffn[0][None, :],
                  s5_w_glu[0].astype(BF16), s5_w_out[0].astype(BF16), w1, w2)

    full_idx, nope_idx, pe_idx = _head_slot_maps()
    w_in = mla_w_in[0]
    w_in_slots = jnp.concatenate(
        [w_in[:, :MLA_QL + MLA_KVL], _take_padded(w_in[:, MLA_QL + MLA_KVL:], pe_idx)], axis=1)
    w_qbt = _take_padded(mla_w_q_b[0].reshape(MLA_QL, MLA_H, MLA_QK), full_idx)
    w_qbt = w_qbt.reshape(MLA_QL, MLA_H * HEAD_SLOT).T
    w_kvb = mla_w_kv_b[0].reshape(MLA_KVL, MLA_H, MLA_NOPE + MLA_V)
    w_kb = _take_padded(w_kvb[:, :, :MLA_NOPE], nope_idx).reshape(MLA_KVL, MLA_H * HEAD_SLOT)
    w_vt = w_kvb[:, :, MLA_NOPE:].reshape(MLA_KVL, MLA_H * MLA_V).T
    cos, sin = _rope_tables(n_ctx, seq)
    q_scale = (MLA_QK ** -0.5) * math.log2(math.e)
    g_q = _take_padded(mla_q_norm[0], full_idx)
    g_k = _take_padded(mla_k_norm[0], full_idx)
    gcq = ((g_q * q_scale) * cos[n_ctx:, :])[:, :MLA_QK].T
    gsq = ((jnp.roll(g_q, ROPE_PAIR) * q_scale) * sin[n_ctx:, :])[:, :MLA_QK].T
    gck = g_k * cos
    gsk = jnp.roll(g_k, ROPE_PAIR) * sin
    score_bound = 1.02 * MLA_QK * q_scale * jnp.max(jnp.abs(mla_q_norm[0])) * jnp.max(jnp.abs(mla_k_norm[0]))
    qt, k, vt = _mla_project(
        xc, modtab[1], norm_mix[1][None, :], w_in_slots.astype(BF16),
        mla_q_a_norm[0][None, :], mla_kv_a_norm[0][None, :],
        w_qbt.astype(BF16), w_kb.astype(BF16), w_vt.astype(BF16),
        gcq, gsq, gck, gsk, n_ctx)
    att = lax.cond(score_bound <= SCORE_BOUND_LIMIT,
                   lambda: _attention(qt, k, vt, True),
                   lambda: _attention(qt, k, vt, False))
    return _mla_tail(xc, att, modtab[1], norm_ffn[1][None, :], mla_w_o[0].astype(BF16), w1, w2,
                     n_ctx)
```

```python
import functools
import math

import jax
import jax.numpy as jnp
import numpy as np
from jax import lax
from jax.experimental import pallas as pl
from jax.experimental.pallas import tpu as pltpu

F32 = jnp.float32
BF16 = jnp.bfloat16

D_MODEL = 1024
EPS = 1e-6
S5_GC = 16
S5_G = D_MODEL // S5_GC
S5_P = 64
S5_STATE = S5_G * S5_P
S5_GROUPS_PER_TILE = 16
S5_TILE_CH = S5_GROUPS_PER_TILE * S5_GC
S5_TILE_ST = S5_GROUPS_PER_TILE * S5_P
S5_NTILES = S5_G // S5_GROUPS_PER_TILE
S5_CHUNK_T = 32
S5_CHUNKS_PER_STEP = 2
MLA_H = 16
MLA_NOPE = 64
MLA_ROPE = 32
MLA_V = 64
MLA_QL = D_MODEL // 2
MLA_KVL = D_MODEL // 4
MLA_QK = MLA_NOPE + MLA_ROPE
HEAD_SLOT = 128
ROPE_PAIR = HEAD_SLOT // 2
ROPE_HALF = MLA_ROPE // 2
SUBLANES = 8
GRID_W = 64
ROPE_BASE = 10000.0
FFN_HIDDEN = 2816
FFN_CHUNK = 256
ROW_BLOCK = 256
ATTN_Q_BLOCK = 256
ATTN_Q_STEP = 4 * ATTN_Q_BLOCK
ATTN_KEY_CHUNK = 1024
HEADS_PER_STEP = 2
SCORE_BOUND_LIMIT = 50.0

VMEM_LIMIT = 56 * 1024 * 1024


def _const_spec(shape):
    nd = len(shape)
    return pl.BlockSpec(shape, lambda *_: (0,) * nd, pipeline_mode=pl.Buffered(1))


def _layer_spec(shape, layer):
    nd = len(shape) - 1
    return pl.BlockSpec((None,) + tuple(shape[1:]), lambda *_: (layer,) + (0,) * nd,
                        pipeline_mode=pl.Buffered(1))


def _rms(x, g):
    ms = jnp.mean(x * x, axis=-1, keepdims=True)
    return x * lax.rsqrt(ms + EPS) * g


def _modnorm(x, g, shift, scale):
    return _rms(x, g) * (1.0 + scale) + shift


def _bdot(a, b):
    return jnp.dot(a, b, preferred_element_type=F32)


def _bdot_nt(a, b):
    return lax.dot_general(a, b, (((1,), (1,)), ((), ())), preferred_element_type=F32)


def _silu(x):
    return x * jax.nn.sigmoid(x)


def _gelu_tanh(x):
    c = math.sqrt(2.0 / math.pi)
    return 0.5 * x * (1.0 + jnp.tanh(c * (x + 0.044715 * (x * x * x))))


def _mod_kernel(c_ref, w_ref, b_ref, o_ref):
    a = _silu(c_ref[...]).astype(BF16)
    o_ref[...] = _bdot(a, w_ref[...].astype(BF16)) + b_ref[...]


def _modulation(cvec, mod_w, mod_b):
    depth, d, n = mod_w.shape
    nb = 1536
    return pl.pallas_call(
        _mod_kernel,
        grid=(depth, n // nb),
        in_specs=[
            pl.BlockSpec((16, d), lambda i, j: (0, 0)),
            pl.BlockSpec((None, d, nb), lambda i, j: (i, 0, j)),
            pl.BlockSpec((None, 1, nb), lambda i, j: (i, 0, j)),
        ],
        out_specs=pl.BlockSpec((None, 16, nb), lambda i, j: (i, 0, j)),
        out_shape=jax.ShapeDtypeStruct((depth, 16, n), F32),
        compiler_params=pltpu.CompilerParams(vmem_limit_bytes=VMEM_LIMIT),
        name="modulation",
    )(cvec, mod_w, mod_b.reshape(depth, 1, n))


def _s5_scan_chunk(u, wb_ref, a_ref, wc_ref, bbuf, hstate, *, reverse):
    rows = u.shape[0]
    nb = hstate.shape[0]
    lc = rows // nb
    ub = u.astype(BF16)

    def expand(j):
        bj = _bdot(ub[:, j * S5_TILE_CH:(j + 1) * S5_TILE_CH], wb_ref[j])
        bbuf[:, j * S5_TILE_ST:(j + 1) * S5_TILE_ST] = bj[:, :S5_TILE_ST]
        bbuf[:, S5_STATE + j * S5_TILE_ST:S5_STATE + (j + 1) * S5_TILE_ST] = bj[:, S5_TILE_ST:]

    ys = []
    expand(0)
    for j in range(S5_NTILES):
        if j + 1 < S5_NTILES:
            expand(j + 1)
        cr = slice(j * S5_TILE_ST, (j + 1) * S5_TILE_ST)
        ci = slice(S5_STATE + j * S5_TILE_ST, S5_STATE + (j + 1) * S5_TILE_ST)
        ar = jnp.broadcast_to(a_ref[0:1, cr], (nb, S5_TILE_ST))
        ai = jnp.broadcast_to(a_ref[1:2, cr], (nb, S5_TILE_ST))
        hr, hi = hstate[:, cr], hstate[:, ci]
        for i in range(lc):
            t = (lc - 1 - i) if reverse else i
            r = slice(t * nb, (t + 1) * nb)
            hr, hi = (ar * hr - ai * hi) + bbuf[r, cr], (ar * hi + ai * hr) + bbuf[r, ci]
            bbuf[r, cr] = hr
            bbuf[r, ci] = hi
        hstate[:, cr] = hr
        hstate[:, ci] = hi
        ys.append(_bdot(bbuf[:, cr].astype(BF16), wc_ref[j, 0:S5_TILE_ST, :])
                  + _bdot(bbuf[:, ci].astype(BF16), wc_ref[j, S5_TILE_ST:, :]))
    return jnp.concatenate(ys, axis=1)


def _s5_zero_state_at_start(hstate):
    @pl.when(pl.program_id(0) == 0)
    def _():
        hstate[...] = jnp.zeros_like(hstate)


def _s5_fwd_kernel(x_ref, ctx_ref, mod_ref, g_ref, win_ref, perm_ref, wb_ref, a_ref, wc_ref,
                   yf_ref, u_ref, *scratch, ctx_steps):
    nb, _, d = x_ref.shape
    lc = S5_CHUNK_T
    bbufs, hstate = scratch[:-1], scratch[-1]
    _s5_zero_state_at_start(hstate)
    is_ctx = pl.program_id(0) < ctx_steps
    us = []
    for sub in range(S5_CHUNKS_PER_STEP):
        ts = slice(sub * lc, (sub + 1) * lc)
        xin = jnp.where(is_ctx, ctx_ref[:, ts, :], x_ref[:, ts, :])
        h = _modnorm(xin, g_ref[...], mod_ref[:, :, 0:d], mod_ref[:, :, d:2 * d])
        hb = h.reshape(nb * lc, d).astype(BF16)
        htb = _bdot(perm_ref[...], hb).astype(BF16)
        us.append(_bdot(htb, win_ref[...]))
        u_ref[sub] = us[sub]
    for sub, bbuf in enumerate(bbufs):
        yf_ref[sub] = _s5_scan_chunk(us[sub], wb_ref, a_ref, wc_ref, bbuf, hstate, reverse=False)


def _s5_bwd_kernel(u_ref, yf_ref, wb_ref, a_ref, wc_ref, dskip_ref, permt_ref, o_ref, *scratch):
    lc = S5_CHUNK_T
    bbufs, hstate = scratch[:-1], scratch[-1]
    _s5_zero_state_at_start(hstate)
    for sub, bbuf in reversed(list(enumerate(bbufs))):
        u = u_ref[sub]
        y = _s5_scan_chunk(u, wb_ref, a_ref, wc_ref, bbuf, hstate, reverse=True)
        y = y + yf_ref[sub] + dskip_ref[...] * u
        gb = _gelu_tanh(y).astype(BF16)
        o_ref[:, sub * lc:(sub + 1) * lc, :] = _bdot(permt_ref[...], gb).astype(BF16).reshape(
            o_ref.shape[0], lc, o_ref.shape[2])


def _s5_mixer(x, ctx, modtab, norm_g, w_in, perm, permt, wb, a, wc, dskip):
    nb, seq, d = x.shape
    n_ctx = ctx.shape[1]
    lc = S5_CHUNK_T
    rows = nb * lc
    cps = S5_CHUNKS_PER_STEP
    ls = cps * lc
    ctx_steps = n_ctx // ls
    nsteps = ctx_steps + seq // ls

    def bwd_step(i):
        return jnp.where(i < ctx_steps, ctx_steps - 1 - i, nsteps - 1 + ctx_steps - i)

    def dir_specs(k):
        return [
            pl.BlockSpec((None, S5_NTILES, S5_TILE_CH, 2 * S5_TILE_ST), lambda i: (k, 0, 0, 0),
                         pipeline_mode=pl.Buffered(1)),
            pl.BlockSpec((None, 2, S5_STATE), lambda i: (k, 0, 0), pipeline_mode=pl.Buffered(1)),
            pl.BlockSpec((None, S5_NTILES, 2 * S5_TILE_ST, S5_TILE_CH), lambda i: (k, 0, 0, 0),
                         pipeline_mode=pl.Buffered(1)),
        ]

    scratch = ([pltpu.VMEM((rows, 2 * S5_STATE), F32)] * cps
               + [pltpu.VMEM((nb, 2 * S5_STATE), F32)])
    params = pltpu.CompilerParams(dimension_semantics=("arbitrary",), vmem_limit_bytes=VMEM_LIMIT)

    chunk_rows = jax.ShapeDtypeStruct((nsteps * cps, rows, d), F32)

    yf, u = pl.pallas_call(
        functools.partial(_s5_fwd_kernel, ctx_steps=ctx_steps),
        grid=(nsteps,),
        in_specs=[
            pl.BlockSpec((nb, ls, d), lambda i: (0, jnp.maximum(i - ctx_steps, 0), 0)),
            pl.BlockSpec((nb, ls, d), lambda i: (0, jnp.minimum(i, ctx_steps - 1), 0)),
            pl.BlockSpec((None, nb, 1, 6 * d), lambda i: ((i >= ctx_steps).astype(jnp.int32), 0, 0, 0)),
            _const_spec((1, d)),
            _const_spec((d, d)),
            _const_spec((rows, rows)),
        ] + dir_specs(0),
        out_specs=[pl.BlockSpec((cps, rows, d), lambda i: (i, 0, 0))] * 2,
        out_shape=[chunk_rows, chunk_rows],
        scratch_shapes=scratch,
        compiler_params=params,
        name="s5_forward",
    )(x, ctx, modtab, norm_g, w_in, perm, wb, a, wc)

    return pl.pallas_call(
        _s5_bwd_kernel,
        grid=(nsteps,),
        in_specs=[pl.BlockSpec((cps, rows, d), lambda i: (bwd_step(i), 0, 0))] * 2 + dir_specs(1) + [
            _const_spec((1, d)),
            _const_spec((rows, rows)),
        ],
        out_specs=pl.BlockSpec((nb, ls, d), lambda i: (0, bwd_step(i), 0)),
        out_shape=jax.ShapeDtypeStruct((nb, n_ctx + seq, d), BF16),
        scratch_shapes=scratch,
        compiler_params=params,
        name="s5_backward",
    )(u, yf, wb, a, wc, dskip, permt)


def _ffn_residual(x1, mod_ref, g_ref, w1_ref, w2_ref):
    d = x1.shape[-1]
    hn = _modnorm(x1, g_ref[...], mod_ref[:, 3 * d:4 * d], mod_ref[:, 4 * d:5 * d]).astype(BF16)

    def up(k):
        lo = k * FFN_CHUNK
        return (_bdot(hn, w1_ref[:, lo:lo + FFN_CHUNK]),
                _bdot(hn, w1_ref[:, FFN_HIDDEN + lo:FFN_HIDDEN + lo + FFN_CHUNK]))

    nchunks = FFN_HIDDEN // FFN_CHUNK
    acc = jnp.zeros(x1.shape, F32)
    nxt = up(0)
    for k in range(nchunks):
        gk, uk = nxt
        if k + 1 < nchunks:
            nxt = up(k + 1)
        acc = acc + _bdot((_silu(gk) * uk).astype(BF16), w2_ref[k * FFN_CHUNK:(k + 1) * FFN_CHUNK, :])
    return x1 + mod_ref[:, 5 * d:6 * d] * acc


def _s5_tail_kernel(x_ref, ctx_ref, gl_ref, mod_ref, gf_ref, wglu_ref, wout_ref, w1_ref, w2_ref,
                    o_ref, *, ctx_blocks):
    d = x_ref.shape[-1]
    xin = jnp.where(pl.program_id(1) < ctx_blocks, ctx_ref[...], x_ref[...])
    z = _bdot(gl_ref[...], wglu_ref[...])
    gated = (z[:, :d] * jax.nn.sigmoid(z[:, d:])).astype(BF16)
    x1 = xin + mod_ref[:, 2 * d:3 * d] * _bdot(gated, wout_ref[...])
    o_ref[...] = _ffn_residual(x1, mod_ref, gf_ref, w1_ref, w2_ref)


def _mla_tail_kernel(x_ref, at_ref, mod_ref, gf_ref, wo_ref, w1_ref, w2_ref, o_ref):
    d = x_ref.shape[-1]
    x1 = x_ref[...] + mod_ref[:, 2 * d:3 * d] * _bdot(at_ref[...], wo_ref[...])
    o_ref[...] = _ffn_residual(x1, mod_ref, gf_ref, w1_ref, w2_ref)


_TAIL_PARAMS = pltpu.CompilerParams(
    dimension_semantics=("arbitrary", "arbitrary"), vmem_limit_bytes=VMEM_LIMIT)


def _s5_tail(x, ctx, gl, modtab, norm_g, w_glu, w_out, w1, w2):
    nb, seq, d = x.shape
    n_ctx = ctx.shape[1]
    rb = ROW_BLOCK
    ctx_blocks = n_ctx // rb
    nblk = ctx_blocks + seq // rb
    tok = pl.BlockSpec((None, rb, d), lambda b, i: (b, i, 0))
    return pl.pallas_call(
        functools.partial(_s5_tail_kernel, ctx_blocks=ctx_blocks),
        grid=(nb, nblk),
        in_specs=[
            pl.BlockSpec((None, rb, d), lambda b, i: (b, jnp.maximum(i - ctx_blocks, 0), 0)),
            pl.BlockSpec((None, rb, d), lambda b, i: (b, jnp.minimum(i, ctx_blocks - 1), 0)),
            tok,
            pl.BlockSpec((None, None, 1, 6 * d),
                         lambda b, i: ((i >= ctx_blocks).astype(jnp.int32), b, 0, 0)),
            _const_spec((1, d)),
            _const_spec(w_glu.shape), _const_spec(w_out.shape),
            _layer_spec(w1.shape, 0), _layer_spec(w2.shape, 0),
        ],
        out_specs=tok,
        out_shape=jax.ShapeDtypeStruct((nb, nblk * rb, d), F32),
        compiler_params=_TAIL_PARAMS,
        name="s5_tail",
    )(x, ctx, gl, modtab, norm_g, w_glu, w_out, w1, w2)


def _mla_tail(xc, att, modtab, norm_g, w_o, w1, w2, n_ctx):
    nb, s, d = xc.shape
    rb = ROW_BLOCK
    off = n_ctx // rb
    nblk = s // rb - off
    tok = pl.BlockSpec((None, rb, d), lambda b, i: (b, i, 0))
    return pl.pallas_call(
        _mla_tail_kernel,
        grid=(nb, nblk),
        in_specs=[
            pl.BlockSpec((None, rb, d), lambda b, i: (b, i + off, 0)),
            tok,
            pl.BlockSpec((None, None, 1, 6 * d), lambda b, i: (1, b, 0, 0)),
            _const_spec((1, d)),
            _const_spec(w_o.shape), _layer_spec(w1.shape, 1), _layer_spec(w2.shape, 1),
        ],
        out_specs=tok,
        out_shape=jax.ShapeDtypeStruct((nb, nblk * rb, d), F32),
        compiler_params=_TAIL_PARAMS,
        name="mla_tail",
    )(xc, att, modtab, norm_g, w_o, w1, w2)


def _mla_proj_kernel(x_ref, mod_ref, g_ref, win_ref, gqa_ref, gkva_ref, wqbt_ref, wkb_ref, wvt_ref,
                     gcq_ref, gsq_ref, gck_ref, gsk_ref,
                     qt_ref, k_ref, vt_ref):
    d = x_ref.shape[-1]
    h = _modnorm(x_ref[...], g_ref[...], mod_ref[:, 0:d], mod_ref[:, d:2 * d]).astype(BF16)
    proj = _bdot(h, win_ref[...])
    inv_qk = 1.0 / MLA_QK
    lo, hi = slice(0, ROPE_HALF), slice(ROPE_PAIR, ROPE_PAIR + ROPE_HALF)

    ql = _rms(proj[:, 0:MLA_QL], gqa_ref[...]).astype(BF16)
    ckv = _rms(proj[:, MLA_QL:MLA_QL + MLA_KVL], gkva_ref[...]).astype(BF16)
    qt = _bdot_nt(wqbt_ref[...], ql)
    knope = _bdot(ckv, wkb_ref[...])
    vt = _bdot_nt(wvt_ref[...], ckv)

    gc, gs = gcq_ref[...], gsq_ref[...]
    pad = jnp.zeros((HEAD_SLOT - MLA_QK, qt.shape[1]), F32)
    for hd in range(MLA_H):
        seg = qt[hd * HEAD_SLOT:hd * HEAD_SLOT + MLA_QK, :]
        r = lax.rsqrt(jnp.sum(seg * seg, axis=0, keepdims=True) * inv_qk + EPS)
        rot = seg * gc
        rot = jnp.concatenate([rot[lo] + seg[hi] * gs[lo], rot[ROPE_HALF:ROPE_PAIR],
                               rot[hi] + seg[lo] * gs[hi], rot[ROPE_PAIR + ROPE_HALF:]], axis=0)
        out = jnp.concatenate([rot * r, pad], axis=0)
        qt_ref[hd * HEAD_SLOT:(hd + 1) * HEAD_SLOT, :] = out.astype(BF16)

    kpe = proj[:, MLA_QL + MLA_KVL:]
    gck = gck_ref[...]
    rot_pe = pltpu.roll(kpe, ROPE_PAIR, 1) * gsk_ref[...]
    for hd in range(MLA_H):
        t = knope[:, hd * HEAD_SLOT:(hd + 1) * HEAD_SLOT] + kpe
        r = lax.rsqrt(jnp.sum(t * t, axis=-1, keepdims=True) * inv_qk + EPS)
        k_ref[:, hd * HEAD_SLOT:(hd + 1) * HEAD_SLOT] = ((t * gck + rot_pe) * r).astype(BF16)

    vt_ref[...] = vt.astype(BF16).reshape(vt_ref.shape)


def _mla_project(xc, modtab, norm_g, w_in, g_qa, g_kva, w_qbt, w_kb, w_vt,
                 gcq, gsq, gck, gsk, n_ctx):
    nb, s, d = xc.shape
    rb = ROW_BLOCK
    ctx_blocks = n_ctx // rb
    qk_n = MLA_H * HEAD_SLOT
    qblk = lambda b, i: (b, 0, jnp.maximum(i - ctx_blocks, 0))
    return pl.pallas_call(
        _mla_proj_kernel,
        grid=(nb, s // rb),
        in_specs=[
            pl.BlockSpec((None, rb, d), lambda b, i: (b, i, 0)),
            pl.BlockSpec((None, None, 1, 6 * d),
                         lambda b, i: ((i >= ctx_blocks).astype(jnp.int32), b, 0, 0)),
            _const_spec((1, d)),
            _const_spec(w_in.shape),
            _const_spec(g_qa.shape),
            _const_spec(g_kva.shape),
            _const_spec(w_qbt.shape),
            _const_spec(w_kb.shape),
            _const_spec(w_vt.shape),
            pl.BlockSpec((MLA_QK, rb), lambda b, i: (0, jnp.maximum(i - ctx_blocks, 0))),
            pl.BlockSpec((MLA_QK, rb), lambda b, i: (0, jnp.maximum(i - ctx_blocks, 0))),
            pl.BlockSpec((rb, HEAD_SLOT), lambda b, i: (i, 0)),
            pl.BlockSpec((rb, HEAD_SLOT), lambda b, i: (i, 0)),
        ],
        out_specs=[
            pl.BlockSpec((None, qk_n, rb), qblk),
            pl.BlockSpec((None, rb, qk_n), lambda b, i: (b, i, 0)),
            pl.BlockSpec((None, MLA_H, MLA_V, rb), lambda b, i: (b, 0, 0, i)),
        ],
        out_shape=[
            jax.ShapeDtypeStruct((nb, qk_n, s - n_ctx), BF16),
            jax.ShapeDtypeStruct((nb, s, qk_n), BF16),
            jax.ShapeDtypeStruct((nb, MLA_H, MLA_V, s), BF16),
        ],
        compiler_params=pltpu.CompilerParams(
            dimension_semantics=("arbitrary", "arbitrary"), vmem_limit_bytes=VMEM_LIMIT),
        name="mla_project",
    )(xc, modtab, norm_g, w_in, g_qa, g_kva, w_qbt, w_kb, w_vt, gcq, gsq, gck, gsk)


def _attn_kernel(qt_ref, k_ref, vt_ref, o_ref, *, bounded):
    tq = ATTN_Q_BLOCK
    nkeys = k_ref.shape[0]
    bounds = list(range(0, nkeys, ATTN_KEY_CHUNK)) + [nkeys]
    nchunks = len(bounds) - 1
    zero = jnp.zeros((HEAD_SLOT, tq), BF16)

    for sub in range(ATTN_Q_STEP // tq):
        qs = slice(sub * tq, (sub + 1) * tq)
        qbd = jnp.concatenate(
            [jnp.concatenate([qt_ref[0:HEAD_SLOT, qs], zero], axis=1),
             jnp.concatenate([zero, qt_ref[HEAD_SLOT:, qs]], axis=1)], axis=0)

        def scores(c, qbd=qbd):
            return _bdot(k_ref[bounds[c]:bounds[c + 1], :], qbd)

        m = jnp.full((1, HEADS_PER_STEP * tq), -jnp.inf, F32)
        l = jnp.zeros((1, HEADS_PER_STEP * tq), F32)
        accs = [jnp.zeros((MLA_V, tq), F32) for _ in range(HEADS_PER_STEP)]
        s_next = scores(0)
        for c in range(nchunks):
            s = s_next
            if c + 1 < nchunks:
                s_next = scores(c + 1)
            if not bounded:
                m_new = jnp.maximum(m, jnp.max(s, axis=0, keepdims=True))
                alpha = jnp.exp2(m - m_new)
                s = s - m_new
                m = m_new
                l = l * alpha
            p = jnp.exp2(s)
            l = l + jnp.sum(p, axis=0, keepdims=True)
            pb = p.astype(BF16)
            for hd in range(HEADS_PER_STEP):
                pv = _bdot(vt_ref[hd, :, bounds[c]:bounds[c + 1]], pb[:, hd * tq:(hd + 1) * tq])
                if bounded:
                    accs[hd] = accs[hd] + pv
                else:
                    accs[hd] = accs[hd] * alpha[:, hd * tq:(hd + 1) * tq] + pv
        inv = 1.0 / l
        outs = [a * inv[:, hd * tq:(hd + 1) * tq] for hd, a in enumerate(accs)]
        o_ref[qs, :] = jnp.concatenate(outs, axis=0).T.astype(BF16)


def _attention(qt, k, vt, bounded):
    nb, _, seq = qt.shape
    s = k.shape[1]
    hp = MLA_H // HEADS_PER_STEP
    return pl.pallas_call(
        functools.partial(_attn_kernel, bounded=bounded),
        grid=(nb, hp, seq // ATTN_Q_STEP),
        in_specs=[
            pl.BlockSpec((None, HEADS_PER_STEP * HEAD_SLOT, ATTN_Q_STEP), lambda b, h, i: (b, h, i)),
            pl.BlockSpec((None, s, HEADS_PER_STEP * HEAD_SLOT), lambda b, h, i: (b, 0, h)),
            pl.BlockSpec((None, HEADS_PER_STEP, MLA_V, s), lambda b, h, i: (b, h, 0, 0)),
        ],
        out_specs=pl.BlockSpec((None, ATTN_Q_STEP, HEADS_PER_STEP * MLA_V), lambda b, h, i: (b, i, h)),
        out_shape=jax.ShapeDtypeStruct((nb, seq, MLA_H * MLA_V), BF16),
        compiler_params=pltpu.CompilerParams(
            dimension_semantics=("arbitrary", "arbitrary", "arbitrary"),
            vmem_limit_bytes=VMEM_LIMIT),
        name="mla_attention_bounded" if bounded else "mla_attention_online",
    )(qt, k, vt)


def _s5_discretize(A_re, A_im, log_step, B_re, B_im):
    lr = -jnp.abs(A_re)
    li = A_im
    dt = jnp.exp(log_step)[..., None]
    mag = jnp.exp(lr * dt)
    ar = mag * jnp.cos(li * dt)
    ai = mag * jnp.sin(li * dt)
    den = lr * lr + li * li
    cr = ((ar - 1) * lr + ai * li) / den
    ci = (ai * lr - (ar - 1) * li) / den
    bbr = cr[..., None] * B_re - ci[..., None] * B_im
    bbi = cr[..., None] * B_im + ci[..., None] * B_re
    return ar, ai, bbr, bbi


def _block_diag_tiles(w, inner):
    nt, rows, _ = w.shape
    gt = S5_GROUPS_PER_TILE
    tiled = jnp.tile(w, (1, 1, gt))
    rg = lax.broadcasted_iota(jnp.int32, (rows, gt * inner), 0) // (rows // gt)
    cg = lax.broadcasted_iota(jnp.int32, (rows, gt * inner), 1) // inner
    return jnp.where((rg == cg)[None], tiled, 0.0)


def _s5_pack(A_re, A_im, log_step, B_re, B_im, C_re, C_im):
    nt = S5_NTILES
    ar, ai, bbr, bbi = _s5_discretize(A_re, A_im, log_step, B_re, B_im)

    def tile_in(w):
        w = w.transpose(0, 1, 3, 2).reshape(2 * nt, S5_TILE_CH, S5_P)
        return _block_diag_tiles(w, S5_P)

    def tile_out(w):
        w = w.transpose(0, 1, 3, 2).reshape(2 * nt, S5_TILE_ST, S5_GC)
        return _block_diag_tiles(w, S5_GC)

    wb = jnp.concatenate([tile_in(bbr), tile_in(bbi)], axis=2).astype(BF16)
    wc = jnp.concatenate([tile_out(C_re), tile_out(-C_im)], axis=1).astype(BF16)
    av = jnp.stack([ar.reshape(2, -1), ai.reshape(2, -1)], axis=1)
    return (wb.reshape(2, nt, S5_TILE_CH, 2 * S5_TILE_ST), av,
            wc.reshape(2, nt, 2 * S5_TILE_ST, S5_TILE_CH))


def _chunk_permutation(nb, lc):
    r = np.arange(nb * lc)
    src = (r % nb) * lc + r // nb
    p = np.zeros((nb * lc, nb * lc), np.float32)
    p[r, src] = 1.0
    return p


def _head_slot_maps():
    full = np.full((HEAD_SLOT,), MLA_QK, np.int32)
    nope = np.full((HEAD_SLOT,), MLA_NOPE, np.int32)
    pe = np.full((HEAD_SLOT,), MLA_ROPE, np.int32)
    for dim in range(MLA_NOPE):
        lane = 16 + dim if dim < 48 else 32 + dim
        full[lane] = dim
        nope[lane] = dim
    half = MLA_ROPE // 4
    for e in range(MLA_ROPE):
        axis, hf, f = e // (2 * half), (e % (2 * half)) // half, e % half
        lane = (ROPE_PAIR if hf == 0 else 0) + axis * half + f
        full[lane] = MLA_NOPE + e
        pe[lane] = e
    return full, nope, pe


def _take_padded(w, idx):
    zero = jnp.zeros(w.shape[:-1] + (1,), w.dtype)
    return jnp.take(jnp.concatenate([w, zero], axis=-1), jnp.asarray(idx), axis=-1)


def _rope_tables(n_ctx, seq):
    half = MLA_ROPE // 4
    rows = seq // GRID_W
    row = jnp.repeat(jnp.arange(rows, dtype=F32), GRID_W)
    col = jnp.tile(jnp.arange(GRID_W, dtype=F32), rows)
    axis_dim = MLA_ROPE // 2
    inv_freq = ROPE_BASE ** (-jnp.arange(0, axis_dim, 2, dtype=F32) / axis_dim)
    ang = jnp.stack([row[:, None] * inv_freq, col[:, None] * inv_freq], axis=1).reshape(seq, 2 * half)
    c, s = jnp.cos(ang), jnp.sin(ang)
    one = jnp.ones((seq, ROPE_PAIR - 2 * half), F32)
    cos = jnp.concatenate([c, one, c, one], axis=1)
    sin = jnp.concatenate([s, 0.0 * one, -s, 0.0 * one], axis=1)
    cos = jnp.concatenate([jnp.ones((n_ctx, HEAD_SLOT), F32), cos], axis=0)
    sin = jnp.concatenate([jnp.zeros((n_ctx, HEAD_SLOT), F32), sin], axis=0)
    return cos, sin


def kernel(x, c, ctx, c_ctx, mod_w, mod_b, norm_mix, norm_ffn, ffn_w_in, ffn_w_out, s5_w_in, s5_A_re, s5_A_im, s5_log_step, s5_B_re, s5_B_im, s5_C_re, s5_C_im, s5_D, s5_w_glu, s5_w_out, mla_w_in, mla_q_a_norm, mla_kv_a_norm, mla_w_q_b, mla_w_kv_b, mla_q_norm, mla_k_norm, mla_w_o):
    nb, seq, d = x.shape
    n_ctx = ctx.shape[1]
    assert d == D_MODEL and nb == SUBLANES
    assert n_ctx % ROW_BLOCK == 0 and seq % ATTN_Q_STEP == 0 and seq % GRID_W == 0

    cvec = jnp.concatenate([c, c_ctx[None, :], jnp.zeros((16 - nb - 1, d), F32)], axis=0)
    mod = _modulation(cvec, mod_w, mod_b)
    mod_x = mod[:, :nb, None, :]
    mod_c = jnp.broadcast_to(mod[:, nb:nb + 1, None, :], mod_x.shape)
    modtab = jnp.stack([mod_c, mod_x], axis=1)

    perm = _chunk_permutation(nb, S5_CHUNK_T)
    wb, av, wc = _s5_pack(s5_A_re[0], s5_A_im[0], s5_log_step[0], s5_B_re[0], s5_B_im[0],
                          s5_C_re[0], s5_C_im[0])
    gl = _s5_mixer(x, ctx, modtab[0], norm_mix[0][None, :], s5_w_in[0].astype(BF16),
                   jnp.asarray(perm, BF16), jnp.asarray(perm.T, BF16), wb, av, wc, s5_D[0][None, :])
    w1 = ffn_w_in.astype(BF16)
    w2 = ffn_w_out.astype(BF16)
    xc = _s5_tail(x, ctx, gl, modtab[0], norm_ffn[0][None, :],
                  s5_w_glu[0].astype(BF16), s5_w_out[0].astype(BF16), w1, w2)

    full_idx, nope_idx, pe_idx = _head_slot_maps()
    w_in = mla_w_in[0]
    w_in_slots = jnp.concatenate(
        [w_in[:, :MLA_QL + MLA_KVL], _take_padded(w_in[:, MLA_QL + MLA_KVL:], pe_idx)], axis=1)
    w_qbt = _take_padded(mla_w_q_b[0].reshape(MLA_QL, MLA_H, MLA_QK), full_idx)
    w_qbt = w_qbt.reshape(MLA_QL, MLA_H * HEAD_SLOT).T
    w_kvb = mla_w_kv_b[0].reshape(MLA_KVL, MLA_H, MLA_NOPE + MLA_V)
    w_kb = _take_padded(w_kvb[:, :, :MLA_NOPE], nope_idx).reshape(MLA_KVL, MLA_H * HEAD_SLOT)
    w_vt = w_kvb[:, :, MLA_NOPE:].reshape(MLA_KVL, MLA_H * MLA_V).T
    cos, sin = _rope_tables(n_ctx, seq)
    q_scale = (MLA_QK ** -0.5) * math.log2(math.e)
    g_q = _take_padded(mla_q_norm[0], full_idx)
    g_k = _take_padded(mla_k_norm[0], full_idx)
    gcq = ((g_q * q_scale) * cos[n_ctx:, :])[:, :MLA_QK].T
    gsq = ((jnp.roll(g_q, ROPE_PAIR) * q_scale) * sin[n_ctx:, :])[:, :MLA_QK].T
    gck = g_k * cos
    gsk = jnp.roll(g_k, ROPE_PAIR) * sin
    score_bound = 1.02 * MLA_QK * q_scale * jnp.max(jnp.abs(mla_q_norm[0])) * jnp.max(jnp.abs(mla_k_norm[0]))
    qt, k, vt = _mla_project(
        xc, modtab[1], norm_mix[1][None, :], w_in_slots.astype(BF16),
        mla_q_a_norm[0][None, :], mla_kv_a_norm[0][None, :],
        w_qbt.astype(BF16), w_kb.astype(BF16), w_vt.astype(BF16),
        gcq, gsq, gck, gsk, n_ctx)
    att = lax.cond(score_bound <= SCORE_BOUND_LIMIT,
                   lambda: _attention(qt, k, vt, True),
                   lambda: _attention(qt, k, vt, False))
    return _mla_tail(xc, att, modtab[1], norm_ffn[1][None, :], mla_w_o[0].astype(BF16), w1, w2,
                     n_ctx)
```

```python
import functools
import math

import jax
import jax.numpy as jnp
import numpy as np
from jax import lax
from jax.experimental import pallas as pl
from jax.experimental.pallas import tpu as pltpu

F32 = jnp.float32
BF16 = jnp.bfloat16

D_MODEL = 1024
EPS = 1e-6
S5_GC = 16
S5_G = D_MODEL // S5_GC
S5_P = 64
S5_STATE = S5_G * S5_P
S5_GROUPS_PER_TILE = 16
S5_TILE_CH = S5_GROUPS_PER_TILE * S5_GC
S5_TILE_ST = S5_GROUPS_PER_TILE * S5_P
S5_NTILES = S5_G // S5_GROUPS_PER_TILE
S5_CHUNK_T = 32
S5_CHUNKS_PER_STEP = 2
MLA_H = 16
MLA_NOPE = 64
MLA_ROPE = 32
MLA_V = 64
MLA_QL = D_MODEL // 2
MLA_KVL = D_MODEL // 4
MLA_QK = MLA_NOPE + MLA_ROPE
HEAD_SLOT = 128
ROPE_PAIR = HEAD_SLOT // 2
ROPE_HALF = MLA_ROPE // 2
SUBLANES = 8
GRID_W = 64
ROPE_BASE = 10000.0
FFN_HIDDEN = 2816
FFN_CHUNK = 256
ROW_BLOCK = 256
ATTN_Q_BLOCK = 256
ATTN_Q_STEP = 8 * ATTN_Q_BLOCK
ATTN_KEY_CHUNK = 1024
HEADS_PER_STEP = 2
SCORE_BOUND_LIMIT = 50.0

VMEM_LIMIT = 56 * 1024 * 1024


def _const_spec(shape):
    nd = len(shape)
    return pl.BlockSpec(shape, lambda *_: (0,) * nd, pipeline_mode=pl.Buffered(1))


def _layer_spec(shape, layer):
    nd = len(shape) - 1
    return pl.BlockSpec((None,) + tuple(shape[1:]), lambda *_: (layer,) + (0,) * nd,
                        pipeline_mode=pl.Buffered(1))


def _rms(x, g):
    ms = jnp.mean(x * x, axis=-1, keepdims=True)
    return x * lax.rsqrt(ms + EPS) * g


def _modnorm(x, g, shift, scale):
    return _rms(x, g) * (1.0 + scale) + shift


def _bdot(a, b):
    return jnp.dot(a, b, preferred_element_type=F32)


def _bdot_nt(a, b):
    return lax.dot_general(a, b, (((1,), (1,)), ((), ())), preferred_element_type=F32)


def _silu(x):
    return x * jax.nn.sigmoid(x)


def _gelu_tanh(x):
    c = math.sqrt(2.0 / math.pi)
    return 0.5 * x * (1.0 + jnp.tanh(c * (x + 0.044715 * (x * x * x))))


def _mod_kernel(c_ref, w_ref, b_ref, o_ref):
    a = _silu(c_ref[...]).astype(BF16)
    o_ref[...] = _bdot(a, w_ref[...].astype(BF16)) + b_ref[...]


def _modulation(cvec, mod_w, mod_b):
    depth, d, n = mod_w.shape
    nb = 1536
    return pl.pallas_call(
        _mod_kernel,
        grid=(depth, n // nb),
        in_specs=[
            pl.BlockSpec((16, d), lambda i, j: (0, 0)),
            pl.BlockSpec((None, d, nb), lambda i, j: (i, 0, j)),
            pl.BlockSpec((None, 1, nb), lambda i, j: (i, 0, j)),
        ],
        out_specs=pl.BlockSpec((None, 16, nb), lambda i, j: (i, 0, j)),
        out_shape=jax.ShapeDtypeStruct((depth, 16, n), F32),
        compiler_params=pltpu.CompilerParams(vmem_limit_bytes=VMEM_LIMIT),
        name="modulation",
    )(cvec, mod_w, mod_b.reshape(depth, 1, n))


def _s5_scan_chunk(u, wb_ref, a_ref, wc_ref, bbuf, hstate, *, reverse):
    rows = u.shape[0]
    nb = hstate.shape[0]
    lc = rows // nb
    ub = u.astype(BF16)

    def expand(j):
        bj = _bdot(ub[:, j * S5_TILE_CH:(j + 1) * S5_TILE_CH], wb_ref[j])
        bbuf[:, j * S5_TILE_ST:(j + 1) * S5_TILE_ST] = bj[:, :S5_TILE_ST]
        bbuf[:, S5_STATE + j * S5_TILE_ST:S5_STATE + (j + 1) * S5_TILE_ST] = bj[:, S5_TILE_ST:]

    ys = []
    expand(0)
    for j in range(S5_NTILES):
        if j + 1 < S5_NTILES:
            expand(j + 1)
        cr = slice(j * S5_TILE_ST, (j + 1) * S5_TILE_ST)
        ci = slice(S5_STATE + j * S5_TILE_ST, S5_STATE + (j + 1) * S5_TILE_ST)
        ar = jnp.broadcast_to(a_ref[0:1, cr], (nb, S5_TILE_ST))
        ai = jnp.broadcast_to(a_ref[1:2, cr], (nb, S5_TILE_ST))
        hr, hi = hstate[:, cr], hstate[:, ci]
        for i in range(lc):
            t = (lc - 1 - i) if reverse else i
            r = slice(t * nb, (t + 1) * nb)
            hr, hi = (ar * hr - ai * hi) + bbuf[r, cr], (ar * hi + ai * hr) + bbuf[r, ci]
            bbuf[r, cr] = hr
            bbuf[r, ci] = hi
        hstate[:, cr] = hr
        hstate[:, ci] = hi
        ys.append(_bdot(bbuf[:, cr].astype(BF16), wc_ref[j, 0:S5_TILE_ST, :])
                  + _bdot(bbuf[:, ci].astype(BF16), wc_ref[j, S5_TILE_ST:, :]))
    return jnp.concatenate(ys, axis=1)


def _s5_zero_state_at_start(hstate):
    @pl.when(pl.program_id(0) == 0)
    def _():
        hstate[...] = jnp.zeros_like(hstate)


def _s5_fwd_kernel(x_ref, ctx_ref, mod_ref, g_ref, win_ref, perm_ref, wb_ref, a_ref, wc_ref,
                   yf_ref, u_ref, *scratch, ctx_steps):
    nb, _, d = x_ref.shape
    lc = S5_CHUNK_T
    bbufs, hstate = scratch[:-1], scratch[-1]
    _s5_zero_state_at_start(hstate)
    is_ctx = pl.program_id(0) < ctx_steps
    us = []
    for sub in range(S5_CHUNKS_PER_STEP):
        ts = slice(sub * lc, (sub + 1) * lc)
        xin = jnp.where(is_ctx, ctx_ref[:, ts, :], x_ref[:, ts, :])
        h = _modnorm(xin, g_ref[...], mod_ref[:, :, 0:d], mod_ref[:, :, d:2 * d])
        hb = h.reshape(nb * lc, d).astype(BF16)
        htb = _bdot(perm_ref[...], hb).astype(BF16)
        us.append(_bdot(htb, win_ref[...]))
        u_ref[sub] = us[sub]
    for sub, bbuf in enumerate(bbufs):
        yf_ref[sub] = _s5_scan_chunk(us[sub], wb_ref, a_ref, wc_ref, bbuf, hstate, reverse=False)


def _s5_bwd_kernel(u_ref, yf_ref, wb_ref, a_ref, wc_ref, dskip_ref, permt_ref, o_ref, *scratch):
    lc = S5_CHUNK_T
    bbufs, hstate = scratch[:-1], scratch[-1]
    _s5_zero_state_at_start(hstate)
    for sub, bbuf in reversed(list(enumerate(bbufs))):
        u = u_ref[sub]
        y = _s5_scan_chunk(u, wb_ref, a_ref, wc_ref, bbuf, hstate, reverse=True)
        y = y + yf_ref[sub] + dskip_ref[...] * u
        gb = _gelu_tanh(y).astype(BF16)
        o_ref[:, sub * lc:(sub + 1) * lc, :] = _bdot(permt_ref[...], gb).astype(BF16).reshape(
            o_ref.shape[0], lc, o_ref.shape[2])


def _s5_mixer(x, ctx, modtab, norm_g, w_in, perm, permt, wb, a, wc, dskip):
    nb, seq, d = x.shape
    n_ctx = ctx.shape[1]
    lc = S5_CHUNK_T
    rows = nb * lc
    cps = S5_CHUNKS_PER_STEP
    ls = cps * lc
    ctx_steps = n_ctx // ls
    nsteps = ctx_steps + seq // ls

    def bwd_step(i):
        return jnp.where(i < ctx_steps, ctx_steps - 1 - i, nsteps - 1 + ctx_steps - i)

    def dir_specs(k):
        return [
            pl.BlockSpec((None, S5_NTILES, S5_TILE_CH, 2 * S5_TILE_ST), lambda i: (k, 0, 0, 0),
                         pipeline_mode=pl.Buffered(1)),
            pl.BlockSpec((None, 2, S5_STATE), lambda i: (k, 0, 0), pipeline_mode=pl.Buffered(1)),
            pl.BlockSpec((None, S5_NTILES, 2 * S5_TILE_ST, S5_TILE_CH), lambda i: (k, 0, 0, 0),
                         pipeline_mode=pl.Buffered(1)),
        ]

    scratch = ([pltpu.VMEM((rows, 2 * S5_STATE), F32)] * cps
               + [pltpu.VMEM((nb, 2 * S5_STATE), F32)])
    params = pltpu.CompilerParams(dimension_semantics=("arbitrary",), vmem_limit_bytes=VMEM_LIMIT)

    chunk_rows = jax.ShapeDtypeStruct((nsteps * cps, rows, d), F32)

    yf, u = pl.pallas_call(
        functools.partial(_s5_fwd_kernel, ctx_steps=ctx_steps),
        grid=(nsteps,),
        in_specs=[
            pl.BlockSpec((nb, ls, d), lambda i: (0, jnp.maximum(i - ctx_steps, 0), 0)),
            pl.BlockSpec((nb, ls, d), lambda i: (0, jnp.minimum(i, ctx_steps - 1), 0)),
            pl.BlockSpec((None, nb, 1, 6 * d), lambda i: ((i >= ctx_steps).astype(jnp.int32), 0, 0, 0)),
            _const_spec((1, d)),
            _const_spec((d, d)),
            _const_spec((rows, rows)),
        ] + dir_specs(0),
        out_specs=[pl.BlockSpec((cps, rows, d), lambda i: (i, 0, 0))] * 2,
        out_shape=[chunk_rows, chunk_rows],
        scratch_shapes=scratch,
        compiler_params=params,
        name="s5_forward",
    )(x, ctx, modtab, norm_g, w_in, perm, wb, a, wc)

    return pl.pallas_call(
        _s5_bwd_kernel,
        grid=(nsteps,),
        in_specs=[pl.BlockSpec((cps, rows, d), lambda i: (bwd_step(i), 0, 0))] * 2 + dir_specs(1) + [
            _const_spec((1, d)),
            _const_spec((rows, rows)),
        ],
        out_specs=pl.BlockSpec((nb, ls, d), lambda i: (0, bwd_step(i), 0)),
        out_shape=jax.ShapeDtypeStruct((nb, n_ctx + seq, d), BF16),
        scratch_shapes=scratch,
        compiler_params=params,
        name="s5_backward",
    )(u, yf, wb, a, wc, dskip, permt)


def _ffn_residual(x1, mod_ref, g_ref, w1_ref, w2_ref):
    d = x1.shape[-1]
    hn = _modnorm(x1, g_ref[...], mod_ref[:, 3 * d:4 * d], mod_ref[:, 4 * d:5 * d]).astype(BF16)

    def up(k):
        lo = k * FFN_CHUNK
        return (_bdot(hn, w1_ref[:, lo:lo + FFN_CHUNK]),
                _bdot(hn, w1_ref[:, FFN_HIDDEN + lo:FFN_HIDDEN + lo + FFN_CHUNK]))

    nchunks = FFN_HIDDEN // FFN_CHUNK
    acc = jnp.zeros(x1.shape, F32)
    nxt = up(0)
    for k in range(nchunks):
        gk, uk = nxt
        if k + 1 < nchunks:
            nxt = up(k + 1)
        acc = acc + _bdot((_silu(gk) * uk).astype(BF16), w2_ref[k * FFN_CHUNK:(k + 1) * FFN_CHUNK, :])
    return x1 + mod_ref[:, 5 * d:6 * d] * acc


def _s5_tail_kernel(x_ref, ctx_ref, gl_ref, mod_ref, gf_ref, wglu_ref, wout_ref, w1_ref, w2_ref,
                    o_ref, *, ctx_blocks):
    d = x_ref.shape[-1]
    xin = jnp.where(pl.program_id(1) < ctx_blocks, ctx_ref[...], x_ref[...])
    z = _bdot(gl_ref[...], wglu_ref[...])
    gated = (z[:, :d] * jax.nn.sigmoid(z[:, d:])).astype(BF16)
    x1 = xin + mod_ref[:, 2 * d:3 * d] * _bdot(gated, wout_ref[...])
    o_ref[...] = _ffn_residual(x1, mod_ref, gf_ref, w1_ref, w2_ref)


def _mla_tail_kernel(x_ref, at_ref, mod_ref, gf_ref, wo_ref, w1_ref, w2_ref, o_ref):
    d = x_ref.shape[-1]
    x1 = x_ref[...] + mod_ref[:, 2 * d:3 * d] * _bdot(at_ref[...], wo_ref[...])
    o_ref[...] = _ffn_residual(x1, mod_ref, gf_ref, w1_ref, w2_ref)


_TAIL_PARAMS = pltpu.CompilerParams(
    dimension_semantics=("arbitrary", "arbitrary"), vmem_limit_bytes=VMEM_LIMIT)


def _s5_tail(x, ctx, gl, modtab, norm_g, w_glu, w_out, w1, w2):
    nb, seq, d = x.shape
    n_ctx = ctx.shape[1]
    rb = ROW_BLOCK
    ctx_blocks = n_ctx // rb
    nblk = ctx_blocks + seq // rb
    tok = pl.BlockSpec((None, rb, d), lambda b, i: (b, i, 0))
    return pl.pallas_call(
        functools.partial(_s5_tail_kernel, ctx_blocks=ctx_blocks),
        grid=(nb, nblk),
        in_specs=[
            pl.BlockSpec((None, rb, d), lambda b, i: (b, jnp.maximum(i - ctx_blocks, 0), 0)),
            pl.BlockSpec((None, rb, d), lambda b, i: (b, jnp.minimum(i, ctx_blocks - 1), 0)),
            tok,
            pl.BlockSpec((None, None, 1, 6 * d),
                         lambda b, i: ((i >= ctx_blocks).astype(jnp.int32), b, 0, 0)),
            _const_spec((1, d)),
            _const_spec(w_glu.shape), _const_spec(w_out.shape),
            _layer_spec(w1.shape, 0), _layer_spec(w2.shape, 0),
        ],
        out_specs=tok,
        out_shape=jax.ShapeDtypeStruct((nb, nblk * rb, d), F32),
        compiler_params=_TAIL_PARAMS,
        name="s5_tail",
    )(x, ctx, gl, modtab, norm_g, w_glu, w_out, w1, w2)


def _mla_tail(xc, att, modtab, norm_g, w_o, w1, w2, n_ctx):
    nb, s, d = xc.shape
    rb = ROW_BLOCK
    off = n_ctx // rb
    nblk = s // rb - off
    tok = pl.BlockSpec((None, rb, d), lambda b, i: (b, i, 0))
    return pl.pallas_call(
        _mla_tail_kernel,
        grid=(nb, nblk),
        in_specs=[
            pl.BlockSpec((None, rb, d), lambda b, i: (b, i + off, 0)),
            tok,
            pl.BlockSpec((None, None, 1, 6 * d), lambda b, i: (1, b, 0, 0)),
            _const_spec((1, d)),
            _const_spec(w_o.shape), _layer_spec(w1.shape, 1), _layer_spec(w2.shape, 1),
        ],
        out_specs=tok,
        out_shape=jax.ShapeDtypeStruct((nb, nblk * rb, d), F32),
        compiler_params=_TAIL_PARAMS,
        name="mla_tail",
    )(xc, att, modtab, norm_g, w_o, w1, w2)


def _mla_proj_kernel(x_ref, mod_ref, g_ref, win_ref, gqa_ref, gkva_ref, wqbt_ref, wkb_ref, wvt_ref,
                     gcq_ref, gsq_ref, gck_ref, gsk_ref,
                     qt_ref, k_ref, vt_ref):
    d = x_ref.shape[-1]
    h = _modnorm(x_ref[...], g_ref[...], mod_ref[:, 0:d], mod_ref[:, d:2 * d]).astype(BF16)
    proj = _bdot(h, win_ref[...])
    inv_qk = 1.0 / MLA_QK
    lo, hi = slice(0, ROPE_HALF), slice(ROPE_PAIR, ROPE_PAIR + ROPE_HALF)

    ql = _rms(proj[:, 0:MLA_QL], gqa_ref[...]).astype(BF16)
    ckv = _rms(proj[:, MLA_QL:MLA_QL + MLA_KVL], gkva_ref[...]).astype(BF16)
    qt = _bdot_nt(wqbt_ref[...], ql)
    knope = _bdot(ckv, wkb_ref[...])
    vt = _bdot_nt(wvt_ref[...], ckv)

    gc, gs = gcq_ref[...], gsq_ref[...]
    pad = jnp.zeros((HEAD_SLOT - MLA_QK, qt.shape[1]), F32)
    for hd in range(MLA_H):
        seg = qt[hd * HEAD_SLOT:hd * HEAD_SLOT + MLA_QK, :]
        r = lax.rsqrt(jnp.sum(seg * seg, axis=0, keepdims=True) * inv_qk + EPS)
        rot = seg * gc
        rot = jnp.concatenate([rot[lo] + seg[hi] * gs[lo], rot[ROPE_HALF:ROPE_PAIR],
                               rot[hi] + seg[lo] * gs[hi], rot[ROPE_PAIR + ROPE_HALF:]], axis=0)
        out = jnp.concatenate([rot * r, pad], axis=0)
        qt_ref[hd * HEAD_SLOT:(hd + 1) * HEAD_SLOT, :] = out.astype(BF16)

    kpe = proj[:, MLA_QL + MLA_KVL:]
    gck = gck_ref[...]
    rot_pe = pltpu.roll(kpe, ROPE_PAIR, 1) * gsk_ref[...]
    for hd in range(MLA_H):
        t = knope[:, hd * HEAD_SLOT:(hd + 1) * HEAD_SLOT] + kpe
        r = lax.rsqrt(jnp.sum(t * t, axis=-1, keepdims=True) * inv_qk + EPS)
        k_ref[:, hd * HEAD_SLOT:(hd + 1) * HEAD_SLOT] = ((t * gck + rot_pe) * r).astype(BF16)

    vt_ref[...] = vt.astype(BF16).reshape(vt_ref.shape)


def _mla_project(xc, modtab, norm_g, w_in, g_qa, g_kva, w_qbt, w_kb, w_vt,
                 gcq, gsq, gck, gsk, n_ctx):
    nb, s, d = xc.shape
    rb = ROW_BLOCK
    ctx_blocks = n_ctx // rb
    qk_n = MLA_H * HEAD_SLOT
    qblk = lambda b, i: (b, 0, jnp.maximum(i - ctx_blocks, 0))
    return pl.pallas_call(
        _mla_proj_kernel,
        grid=(nb, s // rb),
        in_specs=[
            pl.BlockSpec((None, rb, d), lambda b, i: (b, i, 0)),
            pl.BlockSpec((None, None, 1, 6 * d),
                         lambda b, i: ((i >= ctx_blocks).astype(jnp.int32), b, 0, 0)),
            _const_spec((1, d)),
            _const_spec(w_in.shape),
            _const_spec(g_qa.shape),
            _const_spec(g_kva.shape),
            _const_spec(w_qbt.shape),
            _const_spec(w_kb.shape),
            _const_spec(w_vt.shape),
            pl.BlockSpec((MLA_QK, rb), lambda b, i: (0, jnp.maximum(i - ctx_blocks, 0))),
            pl.BlockSpec((MLA_QK, rb), lambda b, i: (0, jnp.maximum(i - ctx_blocks, 0))),
            pl.BlockSpec((rb, HEAD_SLOT), lambda b, i: (i, 0)),
            pl.BlockSpec((rb, HEAD_SLOT), lambda b, i: (i, 0)),
        ],
        out_specs=[
            pl.BlockSpec((None, qk_n, rb), qblk),
            pl.BlockSpec((None, rb, qk_n), lambda b, i: (b, i, 0)),
            pl.BlockSpec((None, MLA_H, MLA_V, rb), lambda b, i: (b, 0, 0, i)),
        ],
        out_shape=[
            jax.ShapeDtypeStruct((nb, qk_n, s - n_ctx), BF16),
            jax.ShapeDtypeStruct((nb, s, qk_n), BF16),
            jax.ShapeDtypeStruct((nb, MLA_H, MLA_V, s), BF16),
        ],
        compiler_params=pltpu.CompilerParams(
            dimension_semantics=("arbitrary", "arbitrary"), vmem_limit_bytes=VMEM_LIMIT),
        name="mla_project",
    )(xc, modtab, norm_g, w_in, g_qa, g_kva, w_qbt, w_kb, w_vt, gcq, gsq, gck, gsk)


def _attn_kernel(qt_ref, k_ref, vt_ref, o_ref, *, bounded):
    tq = ATTN_Q_BLOCK
    nkeys = k_ref.shape[0]
    bounds = list(range(0, nkeys, ATTN_KEY_CHUNK)) + [nkeys]
    nchunks = len(bounds) - 1
    zero = jnp.zeros((HEAD_SLOT, tq), BF16)

    for sub in range(ATTN_Q_STEP // tq):
        qs = slice(sub * tq, (sub + 1) * tq)
        qbd = jnp.concatenate(
            [jnp.concatenate([qt_ref[0:HEAD_SLOT, qs], zero], axis=1),
             jnp.concatenate([zero, qt_ref[HEAD_SLOT:, qs]], axis=1)], axis=0)

        def scores(c, qbd=qbd):
            return _bdot(k_ref[bounds[c]:bounds[c + 1], :], qbd)

        m = jnp.full((1, HEADS_PER_STEP * tq), -jnp.inf, F32)
        l = jnp.zeros((1, HEADS_PER_STEP * tq), F32)
        accs = [jnp.zeros((MLA_V, tq), F32) for _ in range(HEADS_PER_STEP)]
        s_next = scores(0)
        for c in range(nchunks):
            s = s_next
            if c + 1 < nchunks:
                s_next = scores(c + 1)
            if not bounded:
                m_new = jnp.maximum(m, jnp.max(s, axis=0, keepdims=True))
                alpha = jnp.exp2(m - m_new)
                s = s - m_new
                m = m_new
                l = l * alpha
            p = jnp.exp2(s)
            l = l + jnp.sum(p, axis=0, keepdims=True)
            pb = p.astype(BF16)
            for hd in range(HEADS_PER_STEP):
                pv = _bdot(vt_ref[hd, :, bounds[c]:bounds[c + 1]], pb[:, hd * tq:(hd + 1) * tq])
                if bounded:
                    accs[hd] = accs[hd] + pv
                else:
                    accs[hd] = accs[hd] * alpha[:, hd * tq:(hd + 1) * tq] + pv
        inv = 1.0 / l
        outs = [a * inv[:, hd * tq:(hd + 1) * tq] for hd, a in enumerate(accs)]
        o_ref[qs, :] = jnp.concatenate(outs, axis=0).T.astype(BF16)


def _attention(qt, k, vt, bounded):
    nb, _, seq = qt.shape
    s = k.shape[1]
    hp = MLA_H // HEADS_PER_STEP
    return pl.pallas_call(
        functools.partial(_attn_kernel, bounded=bounded),
        grid=(nb, hp, seq // ATTN_Q_STEP),
        in_specs=[
            pl.BlockSpec((None, HEADS_PER_STEP * HEAD_SLOT, ATTN_Q_STEP), lambda b, h, i: (b, h, i)),
            pl.BlockSpec((None, s, HEADS_PER_STEP * HEAD_SLOT), lambda b, h, i: (b, 0, h)),
            pl.BlockSpec((None, HEADS_PER_STEP, MLA_V, s), lambda b, h, i: (b, h, 0, 0)),
        ],
        out_specs=pl.BlockSpec((None, ATTN_Q_STEP, HEADS_PER_STEP * MLA_V), lambda b, h, i: (b, i, h)),
        out_shape=jax.ShapeDtypeStruct((nb, seq, MLA_H * MLA_V), BF16),
        compiler_params=pltpu.CompilerParams(
            dimension_semantics=("arbitrary", "arbitrary", "arbitrary"),
            vmem_limit_bytes=VMEM_LIMIT),
        name="mla_attention_bounded" if bounded else "mla_attention_online",
    )(qt, k, vt)


def _s5_discretize(A_re, A_im, log_step, B_re, B_im):
    lr = -jnp.abs(A_re)
    li = A_im
    dt = jnp.exp(log_step)[..., None]
    mag = jnp.exp(lr * dt)
    ar = mag * jnp.cos(li * dt)
    ai = mag * jnp.sin(li * dt)
    den = lr * lr + li * li
    cr = ((ar - 1) * lr + ai * li) / den
    ci = (ai * lr - (ar - 1) * li) / den
    bbr = cr[..., None] * B_re - ci[..., None] * B_im
    bbi = cr[..., None] * B_im + ci[..., None] * B_re
    return ar, ai, bbr, bbi


def _block_diag_tiles(w, inner):
    nt, rows, _ = w.shape
    gt = S5_GROUPS_PER_TILE
    tiled = jnp.tile(w, (1, 1, gt))
    rg = lax.broadcasted_iota(jnp.int32, (rows, gt * inner), 0) // (rows // gt)
    cg = lax.broadcasted_iota(jnp.int32, (rows, gt * inner), 1) // inner
    return jnp.where((rg == cg)[None], tiled, 0.0)


def _s5_pack(A_re, A_im, log_step, B_re, B_im, C_re, C_im):
    nt = S5_NTILES
    ar, ai, bbr, bbi = _s5_discretize(A_re, A_im, log_step, B_re, B_im)

    def tile_in(w):
        w = w.transpose(0, 1, 3, 2).reshape(2 * nt, S5_TILE_CH, S5_P)
        return _block_diag_tiles(w, S5_P)

    def tile_out(w):
        w = w.transpose(0, 1, 3, 2).reshape(2 * nt, S5_TILE_ST, S5_GC)
        return _block_diag_tiles(w, S5_GC)

    wb = jnp.concatenate([tile_in(bbr), tile_in(bbi)], axis=2).astype(BF16)
    wc = jnp.concatenate([tile_out(C_re), tile_out(-C_im)], axis=1).astype(BF16)
    av = jnp.stack([ar.reshape(2, -1), ai.reshape(2, -1)], axis=1)
    return (wb.reshape(2, nt, S5_TILE_CH, 2 * S5_TILE_ST), av,
            wc.reshape(2, nt, 2 * S5_TILE_ST, S5_TILE_CH))


def _chunk_permutation(nb, lc):
    r = np.arange(nb * lc)
    src = (r % nb) * lc + r // nb
    p = np.zeros((nb * lc, nb * lc), np.float32)
    p[r, src] = 1.0
    return p


def _head_slot_maps():
    full = np.full((HEAD_SLOT,), MLA_QK, np.int32)
    nope = np.full((HEAD_SLOT,), MLA_NOPE, np.int32)
    pe = np.full((HEAD_SLOT,), MLA_ROPE, np.int32)
    for dim in range(MLA_NOPE):
        lane = 16 + dim if dim < 48 else 32 + dim
        full[lane] = dim
        nope[lane] = dim
    half = MLA_ROPE // 4
    for e in range(MLA_ROPE):
        axis, hf, f = e // (2 * half), (e % (2 * half)) // half, e % half
        lane = (ROPE_PAIR if hf == 0 else 0) + axis * half + f
        full[lane] = MLA_NOPE + e
        pe[lane] = e
    return full, nope, pe


def _take_padded(w, idx):
    zero = jnp.zeros(w.shape[:-1] + (1,), w.dtype)
    return jnp.take(jnp.concatenate([w, zero], axis=-1), jnp.asarray(idx), axis=-1)


def _rope_tables(n_ctx, seq):
    half = MLA_ROPE // 4
    rows = seq // GRID_W
    row = jnp.repeat(jnp.arange(rows, dtype=F32), GRID_W)
    col = jnp.tile(jnp.arange(GRID_W, dtype=F32), rows)
    axis_dim = MLA_ROPE // 2
    inv_freq = ROPE_BASE ** (-jnp.arange(0, axis_dim, 2, dtype=F32) / axis_dim)
    ang = jnp.stack([row[:, None] * inv_freq, col[:, None] * inv_freq], axis=1).reshape(seq, 2 * half)
    c, s = jnp.cos(ang), jnp.sin(ang)
    one = jnp.ones((seq, ROPE_PAIR - 2 * half), F32)
    cos = jnp.concatenate([c, one, c, one], axis=1)
    sin = jnp.concatenate([s, 0.0 * one, -s, 0.0 * one], axis=1)
    cos = jnp.concatenate([jnp.ones((n_ctx, HEAD_SLOT), F32), cos], axis=0)
    sin = jnp.concatenate([jnp.zeros((n_ctx, HEAD_SLOT), F32), sin], axis=0)
    return cos, sin


def kernel(x, c, ctx, c_ctx, mod_w, mod_b, norm_mix, norm_ffn, ffn_w_in, ffn_w_out, s5_w_in, s5_A_re, s5_A_im, s5_log_step, s5_B_re, s5_B_im, s5_C_re, s5_C_im, s5_D, s5_w_glu, s5_w_out, mla_w_in, mla_q_a_norm, mla_kv_a_norm, mla_w_q_b, mla_w_kv_b, mla_q_norm, mla_k_norm, mla_w_o):
    nb, seq, d = x.shape
    n_ctx = ctx.shape[1]
    assert d == D_MODEL and nb == SUBLANES
    assert n_ctx % ROW_BLOCK == 0 and seq % ATTN_Q_STEP == 0 and seq % GRID_W == 0

    cvec = jnp.concatenate([c, c_ctx[None, :], jnp.zeros((16 - nb - 1, d), F32)], axis=0)
    mod = _modulation(cvec, mod_w, mod_b)
    mod_x = mod[:, :nb, None, :]
    mod_c = jnp.broadcast_to(mod[:, nb:nb + 1, None, :], mod_x.shape)
    modtab = jnp.stack([mod_c, mod_x], axis=1)

    perm = _chunk_permutation(nb, S5_CHUNK_T)
    wb, av, wc = _s5_pack(s5_A_re[0], s5_A_im[0], s5_log_step[0], s5_B_re[0], s5_B_im[0],
                          s5_C_re[0], s5_C_im[0])
    gl = _s5_mixer(x, ctx, modtab[0], norm_mix[0][None, :], s5_w_in[0].astype(BF16),
                   jnp.asarray(perm, BF16), jnp.asarray(perm.T, BF16), wb, av, wc, s5_D[0][None, :])
    w1 = ffn_w_in.astype(BF16)
    w2 = ffn_w_out.astype(BF16)
    xc = _s5_tail(x, ctx, gl, modtab[0], norm_ffn[0][None, :],
                  s5_w_glu[0].astype(BF16), s5_w_out[0].astype(BF16), w1, w2)

    full_idx, nope_idx, pe_idx = _head_slot_maps()
    w_in = mla_w_in[0]
    w_in_slots = jnp.concatenate(
        [w_in[:, :MLA_QL + MLA_KVL], _take_padded(w_in[:, MLA_QL + MLA_KVL:], pe_idx)], axis=1)
    w_qbt = _take_padded(mla_w_q_b[0].reshape(MLA_QL, MLA_H, MLA_QK), full_idx)
    w_qbt = w_qbt.reshape(MLA_QL, MLA_H * HEAD_SLOT).T
    w_kvb = mla_w_kv_b[0].reshape(MLA_KVL, MLA_H, MLA_NOPE + MLA_V)
    w_kb = _take_padded(w_kvb[:, :, :MLA_NOPE], nope_idx).reshape(MLA_KVL, MLA_H * HEAD_SLOT)
    w_vt = w_kvb[:, :, MLA_NOPE:].reshape(MLA_KVL, MLA_H * MLA_V).T
    cos, sin = _rope_tables(n_ctx, seq)
    q_scale = (MLA_QK ** -0.5) * math.log2(math.e)
    g_q = _take_padded(mla_q_norm[0], full_idx)
    g_k = _take_padded(mla_k_norm[0], full_idx)
    gcq = ((g_q * q_scale) * cos[n_ctx:, :])[:, :MLA_QK].T
    gsq = ((jnp.roll(g_q, ROPE_PAIR) * q_scale) * sin[n_ctx:, :])[:, :MLA_QK].T
    gck = g_k * cos
    gsk = jnp.roll(g_k, ROPE_PAIR) * sin
    score_bound = 1.02 * MLA_QK * q_scale * jnp.max(jnp.abs(mla_q_norm[0])) * jnp.max(jnp.abs(mla_k_norm[0]))
    qt, k, vt = _mla_project(
        xc, modtab[1], norm_mix[1][None, :], w_in_slots.astype(BF16),
        mla_q_a_norm[0][None, :], mla_kv_a_norm[0][None, :],
        w_qbt.astype(BF16), w_kb.astype(BF16), w_vt.astype(BF16),
        gcq, gsq, gck, gsk, n_ctx)
    att = lax.cond(score_bound <= SCORE_BOUND_LIMIT,
                   lambda: _attention(qt, k, vt, True),
                   lambda: _attention(qt, k, vt, False))
    return _mla_tail(xc, att, modtab[1], norm_ffn[1][None, :], mla_w_o[0].astype(BF16), w1, w2,
                     n_ctx)
```

```python
import functools
import math

import jax
import jax.numpy as jnp
import numpy as np
from jax import lax
from jax.experimental import pallas as pl
from jax.experimental.pallas import tpu as pltpu

F32 = jnp.float32
BF16 = jnp.bfloat16

D_MODEL = 1024
EPS = 1e-6
S5_GC = 16
S5_G = D_MODEL // S5_GC
S5_P = 64
S5_STATE = S5_G * S5_P
S5_GROUPS_PER_TILE = 16
S5_TILE_CH = S5_GROUPS_PER_TILE * S5_GC
S5_TILE_ST = S5_GROUPS_PER_TILE * S5_P
S5_NTILES = S5_G // S5_GROUPS_PER_TILE
S5_CHUNK_T = 32
S5_CHUNKS_PER_STEP = 2
MLA_H = 16
MLA_NOPE = 64
MLA_ROPE = 32
MLA_V = 64
MLA_QL = D_MODEL // 2
MLA_KVL = D_MODEL // 4
MLA_QK = MLA_NOPE + MLA_ROPE
HEAD_SLOT = 128
ROPE_PAIR = HEAD_SLOT // 2
ROPE_HALF = MLA_ROPE // 2
SUBLANES = 8
GRID_W = 64
ROPE_BASE = 10000.0
FFN_HIDDEN = 2816
FFN_CHUNK = 256
ROW_BLOCK = 256
ATTN_Q_BLOCK = 256
ATTN_Q_STEP = 8 * ATTN_Q_BLOCK
ATTN_KEY_CHUNK = 1024
HEADS_PER_STEP = 2
SCORE_BOUND_LIMIT = 50.0

VMEM_LIMIT = 56 * 1024 * 1024


def _const_spec(shape):
    nd = len(shape)
    return pl.BlockSpec(shape, lambda *_: (0,) * nd, pipeline_mode=pl.Buffered(1))


def _layer_spec(shape, layer):
    nd = len(shape) - 1
    return pl.BlockSpec((None,) + tuple(shape[1:]), lambda *_: (layer,) + (0,) * nd,
                        pipeline_mode=pl.Buffered(1))


def _rms(x, g):
    ms = jnp.mean(x * x, axis=-1, keepdims=True)
    return x * lax.rsqrt(ms + EPS) * g


def _modnorm(x, g, shift, scale):
    return _rms(x, g) * (1.0 + scale) + shift


def _bdot(a, b):
    return jnp.dot(a, b, preferred_element_type=F32)


def _bdot_nt(a, b):
    return lax.dot_general(a, b, (((1,), (1,)), ((), ())), preferred_element_type=F32)


def _silu(x):
    return x * jax.nn.sigmoid(x)


def _gelu_tanh(x):
    c = math.sqrt(2.0 / math.pi)
    return 0.5 * x * (1.0 + jnp.tanh(c * (x + 0.044715 * (x * x * x))))


def _mod_kernel(c_ref, w_ref, b_ref, o_ref):
    a = _silu(c_ref[...]).astype(BF16)
    o_ref[...] = _bdot(a, w_ref[...].astype(BF16)) + b_ref[...]


def _modulation(cvec, mod_w, mod_b):
    depth, d, n = mod_w.shape
    nb = 1536
    return pl.pallas_call(
        _mod_kernel,
        grid=(depth, n // nb),
        in_specs=[
            pl.BlockSpec((16, d), lambda i, j: (0, 0)),
            pl.BlockSpec((None, d, nb), lambda i, j: (i, 0, j)),
            pl.BlockSpec((None, 1, nb), lambda i, j: (i, 0, j)),
        ],
        out_specs=pl.BlockSpec((None, 16, nb), lambda i, j: (i, 0, j)),
        out_shape=jax.ShapeDtypeStruct((depth, 16, n), F32),
        compiler_params=pltpu.CompilerParams(vmem_limit_bytes=VMEM_LIMIT),
        name="modulation",
    )(cvec, mod_w, mod_b.reshape(depth, 1, n))


def _s5_scan_chunk(u, wb_ref, a_ref, wc_ref, bbuf, hstate, *, reverse):
    rows = u.shape[0]
    nb = hstate.shape[0]
    lc = rows // nb
    ub = u.astype(BF16)

    def expand(j):
        bj = _bdot(ub[:, j * S5_TILE_CH:(j + 1) * S5_TILE_CH], wb_ref[j])
        bbuf[:, j * S5_TILE_ST:(j + 1) * S5_TILE_ST] = bj[:, :S5_TILE_ST]
        bbuf[:, S5_STATE + j * S5_TILE_ST:S5_STATE + (j + 1) * S5_TILE_ST] = bj[:, S5_TILE_ST:]

    ys = []
    expand(0)
    for j in range(S5_NTILES):
        if j + 1 < S5_NTILES:
            expand(j + 1)
        cr = slice(j * S5_TILE_ST, (j + 1) * S5_TILE_ST)
        ci = slice(S5_STATE + j * S5_TILE_ST, S5_STATE + (j + 1) * S5_TILE_ST)
        ar = jnp.broadcast_to(a_ref[0:1, cr], (nb, S5_TILE_ST))
        ai = jnp.broadcast_to(a_ref[1:2, cr], (nb, S5_TILE_ST))
        hr, hi = hstate[:, cr], hstate[:, ci]
        for i in range(lc):
            t = (lc - 1 - i) if reverse else i
            r = slice(t * nb, (t + 1) * nb)
            hr, hi = (ar * hr - ai * hi) + bbuf[r, cr], (ar * hi + ai * hr) + bbuf[r, ci]
            bbuf[r, cr] = hr
            bbuf[r, ci] = hi
        hstate[:, cr] = hr
        hstate[:, ci] = hi
        ys.append(_bdot(bbuf[:, cr].astype(BF16), wc_ref[j, 0:S5_TILE_ST, :])
                  + _bdot(bbuf[:, ci].astype(BF16), wc_ref[j, S5_TILE_ST:, :]))
    return jnp.concatenate(ys, axis=1)


def _s5_zero_state_at_start(hstate):
    @pl.when(pl.program_id(0) == 0)
    def _():
        hstate[...] = jnp.zeros_like(hstate)


def _s5_fwd_kernel(x_ref, ctx_ref, mod_ref, g_ref, win_ref, perm_ref, wb_ref, a_ref, wc_ref,
                   yf_ref, u_ref, *scratch, ctx_steps):
    nb, _, d = x_ref.shape
    lc = S5_CHUNK_T
    bbufs, hstate = scratch[:-1], scratch[-1]
    _s5_zero_state_at_start(hstate)
    is_ctx = pl.program_id(0) < ctx_steps
    us = []
    for sub in range(S5_CHUNKS_PER_STEP):
        ts = slice(sub * lc, (sub + 1) * lc)
        xin = jnp.where(is_ctx, ctx_ref[:, ts, :], x_ref[:, ts, :])
        h = _modnorm(xin, g_ref[...], mod_ref[:, :, 0:d], mod_ref[:, :, d:2 * d])
        hb = h.reshape(nb * lc, d).astype(BF16)
        htb = _bdot(perm_ref[...], hb).astype(BF16)
        us.append(_bdot(htb, win_ref[...]))
        u_ref[sub] = us[sub]
    for sub, bbuf in enumerate(bbufs):
        yf_ref[sub] = _s5_scan_chunk(us[sub], wb_ref, a_ref, wc_ref, bbuf, hstate, reverse=False)


def _s5_bwd_kernel(u_ref, yf_ref, wb_ref, a_ref, wc_ref, dskip_ref, permt_ref, o_ref, *scratch):
    lc = S5_CHUNK_T
    bbufs, hstate = scratch[:-1], scratch[-1]
    _s5_zero_state_at_start(hstate)
    for sub, bbuf in reversed(list(enumerate(bbufs))):
        u = u_ref[sub]
        y = _s5_scan_chunk(u, wb_ref, a_ref, wc_ref, bbuf, hstate, reverse=True)
        y = y + yf_ref[sub] + dskip_ref[...] * u
        gb = _gelu_tanh(y).astype(BF16)
        o_ref[:, sub * lc:(sub + 1) * lc, :] = _bdot(permt_ref[...], gb).astype(BF16).reshape(
            o_ref.shape[0], lc, o_ref.shape[2])


def _s5_mixer(x, ctx, modtab, norm_g, w_in, perm, permt, wb, a, wc, dskip):
    nb, seq, d = x.shape
    n_ctx = ctx.shape[1]
    lc = S5_CHUNK_T
    rows = nb * lc
    cps = S5_CHUNKS_PER_STEP
    ls = cps * lc
    ctx_steps = n_ctx // ls
    nsteps = ctx_steps + seq // ls

    def bwd_step(i):
        return jnp.where(i < ctx_steps, ctx_steps - 1 - i, nsteps - 1 + ctx_steps - i)

    def dir_specs(k):
        return [
            pl.BlockSpec((None, S5_NTILES, S5_TILE_CH, 2 * S5_TILE_ST), lambda i: (k, 0, 0, 0),
                         pipeline_mode=pl.Buffered(1)),
            pl.BlockSpec((None, 2, S5_STATE), lambda i: (k, 0, 0), pipeline_mode=pl.Buffered(1)),
            pl.BlockSpec((None, S5_NTILES, 2 * S5_TILE_ST, S5_TILE_CH), lambda i: (k, 0, 0, 0),
                         pipeline_mode=pl.Buffered(1)),
        ]

    scratch = ([pltpu.VMEM((rows, 2 * S5_STATE), F32)] * cps
               + [pltpu.VMEM((nb, 2 * S5_STATE), F32)])
    params = pltpu.CompilerParams(dimension_semantics=("arbitrary",), vmem_limit_bytes=VMEM_LIMIT)

    chunk_rows = jax.ShapeDtypeStruct((nsteps * cps, rows, d), F32)

    yf, u = pl.pallas_call(
        functools.partial(_s5_fwd_kernel, ctx_steps=ctx_steps),
        grid=(nsteps,),
        in_specs=[
            pl.BlockSpec((nb, ls, d), lambda i: (0, jnp.maximum(i - ctx_steps, 0), 0)),
            pl.BlockSpec((nb, ls, d), lambda i: (0, jnp.minimum(i, ctx_steps - 1), 0)),
            pl.BlockSpec((None, nb, 1, 6 * d), lambda i: ((i >= ctx_steps).astype(jnp.int32), 0, 0, 0)),
            _const_spec((1, d)),
            _const_spec((d, d)),
            _const_spec((rows, rows)),
        ] + dir_specs(0),
        out_specs=[pl.BlockSpec((cps, rows, d), lambda i: (i, 0, 0))] * 2,
        out_shape=[chunk_rows, chunk_rows],
        scratch_shapes=scratch,
        compiler_params=params,
        name="s5_forward",
    )(x, ctx, modtab, norm_g, w_in, perm, wb, a, wc)

    return pl.pallas_call(
        _s5_bwd_kernel,
        grid=(nsteps,),
        in_specs=[pl.BlockSpec((cps, rows, d), lambda i: (bwd_step(i), 0, 0))] * 2 + dir_specs(1) + [
            _const_spec((1, d)),
            _const_spec((rows, rows)),
        ],
        out_specs=pl.BlockSpec((nb, ls, d), lambda i: (0, bwd_step(i), 0)),
        out_shape=jax.ShapeDtypeStruct((nb, n_ctx + seq, d), BF16),
        scratch_shapes=scratch,
        compiler_params=params,
        name="s5_backward",
    )(u, yf, wb, a, wc, dskip, permt)


def _ffn_residual(x1, mod_ref, g_ref, w1_ref, w2_ref):
    d = x1.shape[-1]
    hn = _modnorm(x1, g_ref[...], mod_ref[:, 3 * d:4 * d], mod_ref[:, 4 * d:5 * d]).astype(BF16)

    def up(k):
        lo = k * FFN_CHUNK
        return (_bdot(hn, w1_ref[:, lo:lo + FFN_CHUNK]),
                _bdot(hn, w1_ref[:, FFN_HIDDEN + lo:FFN_HIDDEN + lo + FFN_CHUNK]))

    nchunks = FFN_HIDDEN // FFN_CHUNK
    acc = jnp.zeros(x1.shape, F32)
    nxt = up(0)
    for k in range(nchunks):
        gk, uk = nxt
        if k + 1 < nchunks:
            nxt = up(k + 1)
        acc = acc + _bdot((_silu(gk) * uk).astype(BF16), w2_ref[k * FFN_CHUNK:(k + 1) * FFN_CHUNK, :])
    return x1 + mod_ref[:, 5 * d:6 * d] * acc


def _s5_tail_kernel(x_ref, ctx_ref, gl_ref, mod_ref, gf_ref, wglu_ref, wout_ref, w1_ref, w2_ref,
                    o_ref, *, ctx_blocks):
    d = x_ref.shape[-1]
    xin = jnp.where(pl.program_id(1) < ctx_blocks, ctx_ref[...], x_ref[...])
    z = _bdot(gl_ref[...], wglu_ref[...])
    gated = (z[:, :d] * jax.nn.sigmoid(z[:, d:])).astype(BF16)
    x1 = xin + mod_ref[:, 2 * d:3 * d] * _bdot(gated, wout_ref[...])
    o_ref[...] = _ffn_residual(x1, mod_ref, gf_ref, w1_ref, w2_ref)


def _mla_tail_kernel(x_ref, at_ref, mod_ref, gf_ref, wo_ref, w1_ref, w2_ref, o_ref):
    d = x_ref.shape[-1]
    x1 = x_ref[...] + mod_ref[:, 2 * d:3 * d] * _bdot(at_ref[...], wo_ref[...])
    o_ref[...] = _ffn_residual(x1, mod_ref, gf_ref, w1_ref, w2_ref)


_TAIL_PARAMS = pltpu.CompilerParams(
    dimension_semantics=("arbitrary", "arbitrary"), vmem_limit_bytes=VMEM_LIMIT)


def _s5_tail(x, ctx, gl, modtab, norm_g, w_glu, w_out, w1, w2):
    nb, seq, d = x.shape
    n_ctx = ctx.shape[1]
    rb = ROW_BLOCK
    ctx_blocks = n_ctx // rb
    nblk = ctx_blocks + seq // rb
    tok = pl.BlockSpec((None, rb, d), lambda b, i: (b, i, 0))
    return pl.pallas_call(
        functools.partial(_s5_tail_kernel, ctx_blocks=ctx_blocks),
        grid=(nb, nblk),
        in_specs=[
            pl.BlockSpec((None, rb, d), lambda b, i: (b, jnp.maximum(i - ctx_blocks, 0), 0)),
            pl.BlockSpec((None, rb, d), lambda b, i: (b, jnp.minimum(i, ctx_blocks - 1), 0)),
            tok,
            pl.BlockSpec((None, None, 1, 6 * d),
                         lambda b, i: ((i >= ctx_blocks).astype(jnp.int32), b, 0, 0)),
            _const_spec((1, d)),
            _const_spec(w_glu.shape), _const_spec(w_out.shape),
            _layer_spec(w1.shape, 0), _layer_spec(w2.shape, 0),
        ],
        out_specs=tok,
        out_shape=jax.ShapeDtypeStruct((nb, nblk * rb, d), F32),
        compiler_params=_TAIL_PARAMS,
        name="s5_tail",
    )(x, ctx, gl, modtab, norm_g, w_glu, w_out, w1, w2)


def _mla_tail(xc, att, modtab, norm_g, w_o, w1, w2, n_ctx):
    nb, s, d = xc.shape
    rb = ROW_BLOCK
    off = n_ctx // rb
    nblk = s // rb - off
    tok = pl.BlockSpec((None, rb, d), lambda b, i: (b, i, 0))
    return pl.pallas_call(
        _mla_tail_kernel,
        grid=(nb, nblk),
        in_specs=[
            pl.BlockSpec((None, rb, d), lambda b, i: (b, i + off, 0)),
            tok,
            pl.BlockSpec((None, None, 1, 6 * d), lambda b, i: (1, b, 0, 0)),
            _const_spec((1, d)),
            _const_spec(w_o.shape), _layer_spec(w1.shape, 1), _layer_spec(w2.shape, 1),
        ],
        out_specs=tok,
        out_shape=jax.ShapeDtypeStruct((nb, nblk * rb, d), F32),
        compiler_params=_TAIL_PARAMS,
        name="mla_tail",
    )(xc, att, modtab, norm_g, w_o, w1, w2)


def _mla_proj_kernel(x_ref, mod_ref, g_ref, win_ref, gqa_ref, gkva_ref, wqbt_ref, wkb_ref, wvt_ref,
                     gcq_ref, gsq_ref, gck_ref, gsk_ref,
                     qt_ref, k_ref, vt_ref):
    d = x_ref.shape[-1]
    h = _modnorm(x_ref[...], g_ref[...], mod_ref[:, 0:d], mod_ref[:, d:2 * d]).astype(BF16)
    proj = _bdot(h, win_ref[...])
    inv_qk = 1.0 / MLA_QK
    lo, hi = slice(0, ROPE_HALF), slice(ROPE_PAIR, ROPE_PAIR + ROPE_HALF)

    ql = _rms(proj[:, 0:MLA_QL], gqa_ref[...]).astype(BF16)
    ckv = _rms(proj[:, MLA_QL:MLA_QL + MLA_KVL], gkva_ref[...]).astype(BF16)
    qt = _bdot_nt(wqbt_ref[...], ql)
    knope = _bdot(ckv, wkb_ref[...])
    vt = _bdot_nt(wvt_ref[...], ckv)

    gc, gs = gcq_ref[...], gsq_ref[...]
    pad = jnp.zeros((HEAD_SLOT - MLA_QK, qt.shape[1]), F32)
    for hd in range(MLA_H):
        seg = qt[hd * HEAD_SLOT:hd * HEAD_SLOT + MLA_QK, :]
        r = lax.rsqrt(jnp.sum(seg * seg, axis=0, keepdims=True) * inv_qk + EPS)
        rot = seg * gc
        rot = jnp.concatenate([rot[lo] + seg[hi] * gs[lo], rot[ROPE_HALF:ROPE_PAIR],
                               rot[hi] + seg[lo] * gs[hi], rot[ROPE_PAIR + ROPE_HALF:]], axis=0)
        out = jnp.concatenate([rot * r, pad], axis=0)
        qt_ref[hd * HEAD_SLOT:(hd + 1) * HEAD_SLOT, :] = out.astype(BF16)

    kpe = proj[:, MLA_QL + MLA_KVL:]
    gck = gck_ref[...]
    rot_pe = pltpu.roll(kpe, ROPE_PAIR, 1) * gsk_ref[...]
    for hd in range(MLA_H):
        t = knope[:, hd * HEAD_SLOT:(hd + 1) * HEAD_SLOT] + kpe
        r = lax.rsqrt(jnp.sum(t * t, axis=-1, keepdims=True) * inv_qk + EPS)
        k_ref[:, hd * HEAD_SLOT:(hd + 1) * HEAD_SLOT] = ((t * gck + rot_pe) * r).astype(BF16)

    vt_ref[...] = vt.astype(BF16).reshape(vt_ref.shape)


def _mla_project(xc, modtab, norm_g, w_in, g_qa, g_kva, w_qbt, w_kb, w_vt,
                 gcq, gsq, gck, gsk, n_ctx):
    nb, s, d = xc.shape
    rb = ROW_BLOCK
    ctx_blocks = n_ctx // rb
    qk_n = MLA_H * HEAD_SLOT
    qblk = lambda b, i: (b, 0, jnp.maximum(i - ctx_blocks, 0))
    return pl.pallas_call(
        _mla_proj_kernel,
        grid=(nb, s // rb),
        in_specs=[
            pl.BlockSpec((None, rb, d), lambda b, i: (b, i, 0)),
            pl.BlockSpec((None, None, 1, 6 * d),
                         lambda b, i: ((i >= ctx_blocks).astype(jnp.int32), b, 0, 0)),
            _const_spec((1, d)),
            _const_spec(w_in.shape),
            _const_spec(g_qa.shape),
            _const_spec(g_kva.shape),
            _const_spec(w_qbt.shape),
            _const_spec(w_kb.shape),
            _const_spec(w_vt.shape),
            pl.BlockSpec((MLA_QK, rb), lambda b, i: (0, jnp.maximum(i - ctx_blocks, 0))),
            pl.BlockSpec((MLA_QK, rb), lambda b, i: (0, jnp.maximum(i - ctx_blocks, 0))),
            pl.BlockSpec((rb, HEAD_SLOT), lambda b, i: (i, 0)),
            pl.BlockSpec((rb, HEAD_SLOT), lambda b, i: (i, 0)),
        ],
        out_specs=[
            pl.BlockSpec((None, qk_n, rb), qblk),
            pl.BlockSpec((None, rb, qk_n), lambda b, i: (b, i, 0)),
            pl.BlockSpec((None, MLA_H, MLA_V, rb), lambda b, i: (b, 0, 0, i)),
        ],
        out_shape=[
            jax.ShapeDtypeStruct((nb, qk_n, s - n_ctx), BF16),
            jax.ShapeDtypeStruct((nb, s, qk_n), BF16),
            jax.ShapeDtypeStruct((nb, MLA_H, MLA_V, s), BF16),
        ],
        compiler_params=pltpu.CompilerParams(
            dimension_semantics=("arbitrary", "arbitrary"), vmem_limit_bytes=VMEM_LIMIT),
        name="mla_project",
    )(xc, modtab, norm_g, w_in, g_qa, g_kva, w_qbt, w_kb, w_vt, gcq, gsq, gck, gsk)


def _attn_kernel(qt_ref, k_ref, vt_ref, o_ref, *, bounded):
    tq = ATTN_Q_BLOCK
    nkeys = k_ref.shape[0]
    bounds = list(range(0, nkeys, ATTN_KEY_CHUNK)) + [nkeys]
    nchunks = len(bounds) - 1
    zero = jnp.zeros((HEAD_SLOT, tq), BF16)

    for sub in range(ATTN_Q_STEP // tq):
        qs = slice(sub * tq, (sub + 1) * tq)
        qbd = jnp.concatenate(
            [jnp.concatenate([qt_ref[0:HEAD_SLOT, qs], zero], axis=1),
             jnp.concatenate([zero, qt_ref[HEAD_SLOT:, qs]], axis=1)], axis=0)

        def scores(c, qbd=qbd):
            return _bdot(k_ref[bounds[c]:bounds[c + 1], :], qbd)

        m = jnp.full((1, HEADS_PER_STEP * tq), -jnp.inf, F32)
        l = jnp.zeros((1, HEADS_PER_STEP * tq), F32)
        accs = [jnp.zeros((MLA_V, tq), F32) for _ in range(HEADS_PER_STEP)]
        for c in range(nchunks):
            s = scores(c)
            if not bounded:
                m_new = jnp.maximum(m, jnp.max(s, axis=0, keepdims=True))
                alpha = jnp.exp2(m - m_new)
                s = s - m_new
                m = m_new
                l = l * alpha
            p = jnp.exp2(s)
            l = l + jnp.sum(p, axis=0, keepdims=True)
            pb = p.astype(BF16)
            for hd in range(HEADS_PER_STEP):
                pv = _bdot(vt_ref[hd, :, bounds[c]:bounds[c + 1]], pb[:, hd * tq:(hd + 1) * tq])
                if bounded:
                    accs[hd] = accs[hd] + pv
                else:
                    accs[hd] = accs[hd] * alpha[:, hd * tq:(hd + 1) * tq] + pv
        inv = 1.0 / l
        outs = [a * inv[:, hd * tq:(hd + 1) * tq] for hd, a in enumerate(accs)]
        o_ref[qs, :] = jnp.concatenate(outs, axis=0).T.astype(BF16)


def _attention(qt, k, vt, bounded):
    nb, _, seq = qt.shape
    s = k.shape[1]
    hp = MLA_H // HEADS_PER_STEP
    return pl.pallas_call(
        functools.partial(_attn_kernel, bounded=bounded),
        grid=(nb, hp, seq // ATTN_Q_STEP),
        in_specs=[
            pl.BlockSpec((None, HEADS_PER_STEP * HEAD_SLOT, ATTN_Q_STEP), lambda b, h, i: (b, h, i)),
            pl.BlockSpec((None, s, HEADS_PER_STEP * HEAD_SLOT), lambda b, h, i: (b, 0, h)),
            pl.BlockSpec((None, HEADS_PER_STEP, MLA_V, s), lambda b, h, i: (b, h, 0, 0)),
        ],
        out_specs=pl.BlockSpec((None, ATTN_Q_STEP, HEADS_PER_STEP * MLA_V), lambda b, h, i: (b, i, h)),
        out_shape=jax.ShapeDtypeStruct((nb, seq, MLA_H * MLA_V), BF16),
        compiler_params=pltpu.CompilerParams(
            dimension_semantics=("arbitrary", "arbitrary", "arbitrary"),
            vmem_limit_bytes=VMEM_LIMIT),
        name="mla_attention_bounded" if bounded else "mla_attention_online",
    )(qt, k, vt)


def _s5_discretize(A_re, A_im, log_step, B_re, B_im):
    lr = -jnp.abs(A_re)
    li = A_im
    dt = jnp.exp(log_step)[..., None]
    mag = jnp.exp(lr * dt)
    ar = mag * jnp.cos(li * dt)
    ai = mag * jnp.sin(li * dt)
    den = lr * lr + li * li
    cr = ((ar - 1) * lr + ai * li) / den
    ci = (ai * lr - (ar - 1) * li) / den
    bbr = cr[..., None] * B_re - ci[..., None] * B_im
    bbi = cr[..., None] * B_im + ci[..., None] * B_re
    return ar, ai, bbr, bbi


def _block_diag_tiles(w, inner):
    nt, rows, _ = w.shape
    gt = S5_GROUPS_PER_TILE
    tiled = jnp.tile(w, (1, 1, gt))
    rg = lax.broadcasted_iota(jnp.int32, (rows, gt * inner), 0) // (rows // gt)
    cg = lax.broadcasted_iota(jnp.int32, (rows, gt * inner), 1) // inner
    return jnp.where((rg == cg)[None], tiled, 0.0)


def _s5_pack(A_re, A_im, log_step, B_re, B_im, C_re, C_im):
    nt = S5_NTILES
    ar, ai, bbr, bbi = _s5_discretize(A_re, A_im, log_step, B_re, B_im)

    def tile_in(w):
        w = w.transpose(0, 1, 3, 2).reshape(2 * nt, S5_TILE_CH, S5_P)
        return _block_diag_tiles(w, S5_P)

    def tile_out(w):
        w = w.transpose(0, 1, 3, 2).reshape(2 * nt, S5_TILE_ST, S5_GC)
        return _block_diag_tiles(w, S5_GC)

    wb = jnp.concatenate([tile_in(bbr), tile_in(bbi)], axis=2).astype(BF16)
    wc = jnp.concatenate([tile_out(C_re), tile_out(-C_im)], axis=1).astype(BF16)
    av = jnp.stack([ar.reshape(2, -1), ai.reshape(2, -1)], axis=1)
    return (wb.reshape(2, nt, S5_TILE_CH, 2 * S5_TILE_ST), av,
            wc.reshape(2, nt, 2 * S5_TILE_ST, S5_TILE_CH))


def _chunk_permutation(nb, lc):
    r = np.arange(nb * lc)
    src = (r % nb) * lc + r // nb
    p = np.zeros((nb * lc, nb * lc), np.float32)
    p[r, src] = 1.0
    return p


def _head_slot_maps():
    full = np.full((HEAD_SLOT,), MLA_QK, np.int32)
    nope = np.full((HEAD_SLOT,), MLA_NOPE, np.int32)
    pe = np.full((HEAD_SLOT,), MLA_ROPE, np.int32)
    for dim in range(MLA_NOPE):
        lane = 16 + dim if dim < 48 else 32 + dim
        full[lane] = dim
        nope[lane] = dim
    half = MLA_ROPE // 4
    for e in range(MLA_ROPE):
        axis, hf, f = e // (2 * half), (e % (2 * half)) // half, e % half
        lane = (ROPE_PAIR if hf == 0 else 0) + axis * half + f
        full[lane] = MLA_NOPE + e
        pe[lane] = e
    return full, nope, pe


def _take_padded(w, idx):
    zero = jnp.zeros(w.shape[:-1] + (1,), w.dtype)
    return jnp.take(jnp.concatenate([w, zero], axis=-1), jnp.asarray(idx), axis=-1)


def _rope_tables(n_ctx, seq):
    half = MLA_ROPE // 4
    rows = seq // GRID_W
    row = jnp.repeat(jnp.arange(rows, dtype=F32), GRID_W)
    col = jnp.tile(jnp.arange(GRID_W, dtype=F32), rows)
    axis_dim = MLA_ROPE // 2
    inv_freq = ROPE_BASE ** (-jnp.arange(0, axis_dim, 2, dtype=F32) / axis_dim)
    ang = jnp.stack([row[:, None] * inv_freq, col[:, None] * inv_freq], axis=1).reshape(seq, 2 * half)
    c, s = jnp.cos(ang), jnp.sin(ang)
    one = jnp.ones((seq, ROPE_PAIR - 2 * half), F32)
    cos = jnp.concatenate([c, one, c, one], axis=1)
    sin = jnp.concatenate([s, 0.0 * one, -s, 0.0 * one], axis=1)
    cos = jnp.concatenate([jnp.ones((n_ctx, HEAD_SLOT), F32), cos], axis=0)
    sin = jnp.concatenate([jnp.zeros((n_ctx, HEAD_SLOT), F32), sin], axis=0)
    return cos, sin


def kernel(x, c, ctx, c_ctx, mod_w, mod_b, norm_mix, norm_ffn, ffn_w_in, ffn_w_out, s5_w_in, s5_A_re, s5_A_im, s5_log_step, s5_B_re, s5_B_im, s5_C_re, s5_C_im, s5_D, s5_w_glu, s5_w_out, mla_w_in, mla_q_a_norm, mla_kv_a_norm, mla_w_q_b, mla_w_kv_b, mla_q_norm, mla_k_norm, mla_w_o):
    nb, seq, d = x.shape
    n_ctx = ctx.shape[1]
    assert d == D_MODEL and nb == SUBLANES
    assert n_ctx % ROW_BLOCK == 0 and seq % ATTN_Q_STEP == 0 and seq % GRID_W == 0

    cvec = jnp.concatenate([c, c_ctx[None, :], jnp.zeros((16 - nb - 1, d), F32)], axis=0)
    mod = _modulation(cvec, mod_w, mod_b)
    mod_x = mod[:, :nb, None, :]
    mod_c = jnp.broadcast_to(mod[:, nb:nb + 1, None, :], mod_x.shape)
    modtab = jnp.stack([mod_c, mod_x], axis=1)

    perm = _chunk_permutation(nb, S5_CHUNK_T)
    wb, av, wc = _s5_pack(s5_A_re[0], s5_A_im[0], s5_log_step[0], s5_B_re[0], s5_B_im[0],
                          s5_C_re[0], s5_C_im[0])
    gl = _s5_mixer(x, ctx, modtab[0], norm_mix[0][None, :], s5_w_in[0].astype(BF16),
                   jnp.asarray(perm, BF16), jnp.asarray(perm.T, BF16), wb, av, wc, s5_D[0][None, :])
    w1 = ffn_w_in.astype(BF16)
    w2 = ffn_w_out.astype(BF16)
    xc = _s5_tail(x, ctx, gl, modtab[0], norm_ffn[0][None, :],
                  s5_w_glu[0].astype(BF16), s5_w_out[0].astype(BF16), w1, w2)

    full_idx, nope_idx, pe_idx = _head_slot_maps()
    w_in = mla_w_in[0]
    w_in_slots = jnp.concatenate(
        [w_in[:, :MLA_QL + MLA_KVL], _take_padded(w_in[:, MLA_QL + MLA_KVL:], pe_idx)], axis=1)
    w_qbt = _take_padded(mla_w_q_b[0].reshape(MLA_QL, MLA_H, MLA_QK), full_idx)
    w_qbt = w_qbt.reshape(MLA_QL, MLA_H * HEAD_SLOT).T
    w_kvb = mla_w_kv_b[0].reshape(MLA_KVL, MLA_H, MLA_NOPE + MLA_V)
    w_kb = _take_padded(w_kvb[:, :, :MLA_NOPE], nope_idx).reshape(MLA_KVL, MLA_H * HEAD_SLOT)
    w_vt = w_kvb[:, :, MLA_NOPE:].reshape(MLA_KVL, MLA_H * MLA_V).T
    cos, sin = _rope_tables(n_ctx, seq)
    q_scale = (MLA_QK ** -0.5) * math.log2(math.e)
    g_q = _take_padded(mla_q_norm[0], full_idx)
    g_k = _take_padded(mla_k_norm[0], full_idx)
    gcq = ((g_q * q_scale) * cos[n_ctx:, :])[:, :MLA_QK].T
    gsq = ((jnp.roll(g_q, ROPE_PAIR) * q_scale) * sin[n_ctx:, :])[:, :MLA_QK].T
    gck = g_k * cos
    gsk = jnp.roll(g_k, ROPE_PAIR) * sin
    score_bound = 1.02 * MLA_QK * q_scale * jnp.max(jnp.abs(mla_q_norm[0])) * jnp.max(jnp.abs(mla_k_norm[0]))
    qt, k, vt = _mla_project(
        xc, modtab[1], norm_mix[1][None, :], w_in_slots.astype(BF16),
        mla_q_a_norm[0][None, :], mla_kv_a_norm[0][None, :],
        w_qbt.astype(BF16), w_kb.astype(BF16), w_vt.astype(BF16),
        gcq, gsq, gck, gsk, n_ctx)
    att = lax.cond(score_bound <= SCORE_BOUND_LIMIT,
                   lambda: _attention(qt, k, vt, True),
                   lambda: _attention(qt, k, vt, False))
    return _mla_tail(xc, att, modtab[1], norm_ffn[1][None, :], mla_w_o[0].astype(BF16), w1, w2,
                     n_ctx)
```
